```python
import math
import functools
import jax, jax.numpy as jnp
from jax import lax
import numpy as np

D_MODEL = 1024
BATCH = 8
SEQ = 2048
DEPTH = 1
DEC_BATCH = 128
DEC_SEQ = 1
PAST_LEN = 8192
PAGE_SIZE = 128

MLA_HEADS = 16
Q_LORA = 384
KV_LORA = 256
NOPE_DIM = 64
ROPE_DIM = 32
V_DIM = 64
ROPE_THETA = 10000.0
Q_BLOCK = 128
SM_SCALE = 1.0 / math.sqrt(NOPE_DIM + ROPE_DIM)
SSM_EXPAND = 2
D_INNER = SSM_EXPAND * D_MODEL
SSM_HEAD_DIM = 64
SSM_HEADS = D_INNER // SSM_HEAD_DIM
SSM_GROUPS = 8
HEADS_PER_GROUP = SSM_HEADS // SSM_GROUPS
D_STATE = 128
CONV_W = 4
CONV_DIM = D_INNER + 2 * SSM_GROUPS * D_STATE
SSD_CHUNK = 128
D_FF = 4 * D_MODEL
EPS = 1e-6
IN_SPLITS = (Q_LORA, KV_LORA, ROPE_DIM, D_INNER, CONV_DIM, SSM_HEADS, D_MODEL, D_MODEL)
D_IN_PROJ = sum(IN_SPLITS)

kernel_name = "hybrid_mla_ssd_gated_decode_step"


def _rms_scale(x):
    xf = x.astype(jnp.float32)
    return xf * lax.rsqrt(jnp.mean(xf * xf, axis=-1, keepdims=True) + EPS)


def _rmsnorm(x, g):
    return (_rms_scale(x) * g.astype(jnp.float32)).astype(x.dtype)


def _split_in(p):
    bounds = [int(v) for v in np.cumsum(IN_SPLITS)[:-1]]
    return jnp.split(p, bounds, axis=-1)


def _rope(x, pos):
    half = ROPE_DIM // 2
    inv_freq = ROPE_THETA ** (-jnp.arange(half, dtype=jnp.float32) / half)
    ang = pos.astype(jnp.float32)[:, None] * inv_freq[None, :]
    cos = jnp.cos(ang)[None, :, None, :]
    sin = jnp.sin(ang)[None, :, None, :]
    xf = x.astype(jnp.float32)
    x1, x2 = xf[..., :half], xf[..., half:]
    return jnp.concatenate([x1 * cos - x2 * sin, x1 * sin + x2 * cos], axis=-1).astype(x.dtype)


def _mla_project(h_q, h_kv, h_pe, pos, g_q, w_uq, g_kv):
    cq = _rmsnorm(h_q, g_q)
    q = jnp.einsum("bsr,rhd->bshd", cq, w_uq)
    q_nope = q[..., :NOPE_DIM]
    q_pe = _rope(q[..., NOPE_DIM:], pos)
    c_kv = _rmsnorm(h_kv, g_kv)
    k_pe = _rope(h_pe[:, :, None, :], pos)[:, :, 0, :]
    return q_nope, q_pe, c_kv, k_pe


def _mla_attend_prompt(q_nope, q_pe, c_kv, k_pe, w_uk, w_uv):
    b, s = q_nope.shape[:2]
    n_blk = s // Q_BLOCK
    k_nope = jnp.einsum("bkr,rhd->bkhd", c_kv, w_uk)
    v = jnp.einsum("bkr,rhd->bkhd", c_kv, w_uv)
    key_pos = jnp.arange(s)
    qn = jnp.moveaxis(q_nope.reshape(b, n_blk, Q_BLOCK, MLA_HEADS, NOPE_DIM), 1, 0)
    qp = jnp.moveaxis(q_pe.reshape(b, n_blk, Q_BLOCK, MLA_HEADS, ROPE_DIM), 1, 0)

    def one_block(args):
        qn_b, qp_b, blk = args
        sc = jnp.einsum("bqhd,bkhd->bhqk", qn_b, k_nope) + jnp.einsum("bqhd,bkd->bhqk", qp_b, k_pe)
        q_pos = blk * Q_BLOCK + jnp.arange(Q_BLOCK)
        sc = jnp.where(key_pos[None, :] <= q_pos[:, None], sc.astype(jnp.float32) * SM_SCALE, -jnp.inf)
        p = jax.nn.softmax(sc, axis=-1).astype(v.dtype)
        return jnp.einsum("bhqk,bkhd->bqhd", p, v)

    o = lax.map(one_block, (qn, qp, jnp.arange(n_blk)))
    return jnp.moveaxis(o, 0, 1).reshape(b, s, MLA_HEADS * V_DIM)


def _mla_attend_sample(q_nope, q_pe, c_kv, k_pe, cache_kv, cache_pe, page_table, w_uk, w_uv):
    b, s = q_nope.shape[:2]
    past_kv = cache_kv[page_table].reshape(b, -1, KV_LORA)
    past_pe = cache_pe[page_table].reshape(b, -1, ROPE_DIM)
    past_len = past_kv.shape[1]
    q_lat = jnp.einsum("bshd,rhd->bshr", q_nope, w_uk)
    sc_past = jnp.einsum("bshr,btr->bhst", q_lat, past_kv) + jnp.einsum("bshd,btd->bhst", q_pe, past_pe)
    sc_new = jnp.einsum("bshr,btr->bhst", q_lat, c_kv) + jnp.einsum("bshd,btd->bhst", q_pe, k_pe)
    causal = jnp.arange(s)[None, :] <= jnp.arange(s)[:, None]
    sc_new = jnp.where(causal, sc_new.astype(jnp.float32) * SM_SCALE, -jnp.inf)
    sc = jnp.concatenate([sc_past.astype(jnp.float32) * SM_SCALE, sc_new], axis=-1)
    p = jax.nn.softmax(sc, axis=-1).astype(c_kv.dtype)
    o_lat = (jnp.einsum("bhst,btr->bshr", p[..., :past_len], past_kv)
             + jnp.einsum("bhst,btr->bshr", p[..., past_len:], c_kv))
    return jnp.einsum("bshr,rhd->bshd", o_lat, w_uv).reshape(b, s, MLA_HEADS * V_DIM)


def _causal_conv(xbc, conv_prev, w_conv, b_conv):
    s = xbc.shape[1]
    full = jnp.concatenate([conv_prev.astype(xbc.dtype), xbc], axis=1)
    acc = b_conv
    for k in range(CONV_W):
        acc = acc + full[:, k:k + s] * w_conv[k]
    return jax.nn.silu(acc), full[:, s:]


def _ssd(x, dt, a, bm, cm, init_state):
    b, s = x.shape[:2]
    t = SSD_CHUNK if s % SSD_CHUNK == 0 else s
    nc = s // t
    G, HG, P, N = SSM_GROUPS, HEADS_PER_GROUP, SSM_HEAD_DIM, D_STATE
    la = (dt * a).reshape(b, nc, t, SSM_HEADS)
    xdt = (x.astype(jnp.float32) * dt[..., None]).reshape(b, nc, t, G, HG, P)
    bc = bm.astype(jnp.float32).reshape(b, nc, t, G, N)
    cc = cm.astype(jnp.float32).reshape(b, nc, t, G, N)
    cs = jnp.cumsum(la, axis=2)
    seg = cs[:, :, :, None, :] - cs[:, :, None, :, :]
    tri = jnp.arange(t)[:, None] >= jnp.arange(t)[None, :]
    decay = jnp.exp(jnp.where(tri[:, :, None], seg, -jnp.inf)).reshape(b, nc, t, t, G, HG)
    cb = jnp.einsum("bclgn,bcsgn->bclsg", cc, bc)
    y_diag = jnp.einsum("bclsgh,bcsghp->bclghp", cb[..., None] * decay, xdt)
    to_end = jnp.exp(cs[:, :, -1:, :] - cs).reshape(b, nc, t, G, HG)
    local = jnp.einsum("bcsgn,bcsgh,bcsghp->bcghpn", bc, to_end, xdt)
    chunk_decay = jnp.exp(cs[:, :, -1, :]).reshape(b, nc, G, HG)

    def step(state, inp):
        dec, loc = inp
        return state * dec[..., None, None] + loc, state

    init = init_state.astype(jnp.float32).reshape(b, G, HG, P, N)
    final, prev = lax.scan(step, init, (jnp.moveaxis(chunk_decay, 1, 0), jnp.moveaxis(local, 1, 0)))
    prev = jnp.moveaxis(prev, 0, 1)
    y_off = jnp.einsum("bclgn,bcghpn,bclgh->bclghp", cc, prev, jnp.exp(cs).reshape(b, nc, t, G, HG))
    y = (y_diag + y_off).reshape(b, s, SSM_HEADS, P)
    return y, final.reshape(b, SSM_HEADS, P, N)


def _ssm_branch(z, xbc_raw, dt_raw, conv_prev, ssm_prev, w_conv, b_conv, dt_bias, a_log, d_skip, g_ssm):
    b, s = z.shape[:2]
    xbc, conv_new = _causal_conv(xbc_raw, conv_prev, w_conv, b_conv)
    xs, bm, cm = jnp.split(xbc, [D_INNER, D_INNER + SSM_GROUPS * D_STATE], axis=-1)
    xs = xs.reshape(b, s, SSM_HEADS, SSM_HEAD_DIM)
    bm = bm.reshape(b, s, SSM_GROUPS, D_STATE)
    cm = cm.reshape(b, s, SSM_GROUPS, D_STATE)
    dt = jax.nn.softplus(dt_raw.astype(jnp.float32) + dt_bias.astype(jnp.float32))
    a = -jnp.exp(a_log.astype(jnp.float32))
    y, ssm_new = _ssd(xs, dt, a, bm, cm, ssm_prev)
    y = y + d_skip.astype(jnp.float32)[:, None] * xs.astype(jnp.float32)
    y = y.reshape(b, s, D_INNER) * jax.nn.silu(z.astype(jnp.float32))
    y = _rms_scale(y.reshape(b, s, SSM_GROUPS, D_INNER // SSM_GROUPS)).reshape(b, s, D_INNER)
    return (y * g_ssm.astype(jnp.float32)).astype(z.dtype), conv_new, ssm_new


def _trunk_layer(x, pos, conv_prev, ssm_prev, attend, p):
    h = _rmsnorm(x, p["g_pre_mix"])
    h_q, h_kv, h_pe, z, xbc_raw, dt_raw, gate_a, gate_b = _split_in(h @ p["w_in"])
    q_nope, q_pe, c_kv, k_pe = _mla_project(h_q, h_kv, h_pe, pos, p["g_q"], p["w_uq"], p["g_kv"])
    o_attn = attend(q_nope, q_pe, c_kv, k_pe)
    o_ssm, conv_new, ssm_new = _ssm_branch(z, xbc_raw, dt_raw, conv_prev, ssm_prev, p["w_conv"], p["b_conv"],
                                           p["dt_bias"], p["a_log"], p["d_skip"], p["g_ssm"])
    merged = jax.nn.sigmoid(gate_a) * (o_attn @ p["w_br_attn"]) + jax.nn.sigmoid(gate_b) * (o_ssm @ p["w_br_ssm"])
    x = x + _rmsnorm(merged @ p["w_out"], p["g_post_mix"])
    h2 = _rmsnorm(x, p["g_pre_mlp"])
    f = jnp.square(jax.nn.relu(h2 @ p["w_up"])) @ p["w_down"]
    x = x + _rmsnorm(f, p["g_post_mlp"])
    return x, c_kv, k_pe, conv_new, ssm_new


def setup_inputs(seed: int = 0) -> dict:
    key = jax.random.key(seed)
    ks = jax.random.split(key, 32)
    f32 = jnp.float32
    n_pages = PAST_LEN // PAGE_SIZE
    n_used = DEC_BATCH * n_pages
    n_phys = n_used + n_used // 4

    def nrm(k, shape, scale):
        return jax.random.normal(k, shape, f32) * scale

    def gain(k, shape):
        return 1.0 + 0.05 * jax.random.normal(k, shape, f32)

    dt0 = jnp.exp(jax.random.uniform(ks[20], (DEPTH, SSM_HEADS), f32, math.log(1e-3), math.log(1e-1)))
    return {
        "x_prompt": nrm(ks[0], (BATCH, SEQ, D_MODEL), 1.0),
        "x_sample": nrm(ks[1], (DEC_BATCH, DEC_SEQ, D_MODEL), 1.0),
        "cache_kv_latent": nrm(ks[2], (DEPTH, n_phys, PAGE_SIZE, KV_LORA), 1.0),
        "cache_k_rope": nrm(ks[3], (DEPTH, n_phys, PAGE_SIZE, ROPE_DIM), 1.0),
        "state_ssm": nrm(ks[4], (DEPTH, DEC_BATCH, SSM_HEADS, SSM_HEAD_DIM, D_STATE), 0.1),
        "state_conv": nrm(ks[5], (DEPTH, DEC_BATCH, CONV_W - 1, CONV_DIM), 1.0),
        "page_table": jax.random.permutation(ks[6], n_phys)[:n_used].reshape(DEC_BATCH, n_pages).astype(jnp.int32),
        "w_in": nrm(ks[7], (DEPTH, D_MODEL, D_IN_PROJ), D_MODEL ** -0.5),
        "g_pre_mix": gain(ks[8], (DEPTH, D_MODEL)),
        "g_q": gain(ks[9], (DEPTH, Q_LORA)),
        "w_uq": nrm(ks[10], (DEPTH, Q_LORA, MLA_HEADS, NOPE_DIM + ROPE_DIM), Q_LORA ** -0.5),
        "g_kv": gain(ks[11], (DEPTH, KV_LORA)),
        "w_uk": nrm(ks[12], (DEPTH, KV_LORA, MLA_HEADS, NOPE_DIM), KV_LORA ** -0.5),
        "w_uv": nrm(ks[13], (DEPTH, KV_LORA, MLA_HEADS, V_DIM), KV_LORA ** -0.5),
        "w_conv": nrm(ks[14], (DEPTH, CONV_W, CONV_DIM), CONV_W ** -0.5),
        "b_conv": nrm(ks[15], (DEPTH, CONV_DIM), 0.02),
        "dt_bias": dt0 + jnp.log(-jnp.expm1(-dt0)),
        "a_log": jnp.log(jax.random.uniform(ks[16], (DEPTH, SSM_HEADS), f32, 1.0, 16.0)),
        "d_skip": gain(ks[17], (DEPTH, SSM_HEADS)),
        "g_ssm": gain(ks[18], (DEPTH, D_INNER)),
        "w_br_attn": nrm(ks[19], (DEPTH, MLA_HEADS * V_DIM, D_MODEL), (MLA_HEADS * V_DIM) ** -0.5),
        "w_br_ssm": nrm(ks[21], (DEPTH, D_INNER, D_MODEL), D_INNER ** -0.5),
        "w_out": nrm(ks[22], (DEPTH, D_MODEL, D_MODEL), D_MODEL ** -0.5),
        "g_post_mix": gain(ks[23], (DEPTH, D_MODEL)),
        "g_pre_mlp": gain(ks[24], (DEPTH, D_MODEL)),
        "w_up": nrm(ks[25], (DEPTH, D_MODEL, D_FF), D_MODEL ** -0.5),
        "w_down": nrm(ks[26], (DEPTH, D_FF, D_MODEL), D_FF ** -0.5),
        "g_post_mlp": gain(ks[27], (DEPTH, D_MODEL)),
    }


def reference(x_prompt, x_sample, cache_kv_latent, cache_k_rope, state_ssm, state_conv, page_table,
              w_in, g_pre_mix, g_q, w_uq, g_kv, w_uk, w_uv, w_conv, b_conv, dt_bias, a_log, d_skip, g_ssm,
              w_br_attn, w_br_ssm, w_out, g_post_mix, g_pre_mlp, w_up, w_down, g_post_mlp):
    b_p, s_p = x_prompt.shape[:2]
    b_s, s_s = x_sample.shape[:2]
    past_len = page_table.shape[1] * PAGE_SIZE
    pos_prompt = jnp.arange(s_p, dtype=jnp.int32)
    pos_sample = past_len + jnp.arange(s_s, dtype=jnp.int32)
    conv_zero = jnp.zeros((b_p, CONV_W - 1, CONV_DIM), x_prompt.dtype)
    ssm_zero = jnp.zeros((b_p, SSM_HEADS, SSM_HEAD_DIM, D_STATE), jnp.float32)

    kvp, kpp, ssp, cvp, kvs, kps, sss, cvs = [], [], [], [], [], [], [], []
    xp, xs = x_prompt, x_sample
    for l in range(DEPTH):
        p = dict(w_in=w_in[l], g_pre_mix=g_pre_mix[l], g_q=g_q[l], w_uq=w_uq[l], g_kv=g_kv[l],
                 w_conv=w_conv[l], b_conv=b_conv[l], dt_bias=dt_bias[l], a_log=a_log[l], d_skip=d_skip[l],
                 g_ssm=g_ssm[l], w_br_attn=w_br_attn[l], w_br_ssm=w_br_ssm[l], w_out=w_out[l],
                 g_post_mix=g_post_mix[l], g_pre_mlp=g_pre_mlp[l], w_up=w_up[l], w_down=w_down[l],
                 g_post_mlp=g_post_mlp[l])
        attend_p = functools.partial(_mla_attend_prompt, w_uk=w_uk[l], w_uv=w_uv[l])
        attend_s = functools.partial(_mla_attend_sample, cache_kv=cache_kv_latent[l], cache_pe=cache_k_rope[l],
                                     page_table=page_table, w_uk=w_uk[l], w_uv=w_uv[l])
        xp, ckv_p, kpe_p, conv_p, ssm_p = _trunk_layer(xp, pos_prompt, conv_zero, ssm_zero, attend_p, p)
        xs, ckv_s, kpe_s, conv_s, ssm_s = _trunk_layer(xs, pos_sample, state_conv[l], state_ssm[l], attend_s, p)
        kvp.append(ckv_p); kpp.append(kpe_p); ssp.append(ssm_p); cvp.append(conv_p)
        kvs.append(ckv_s); kps.append(kpe_s); sss.append(ssm_s); cvs.append(conv_s)

    return (xp, xs,
            jnp.stack(kvp), jnp.stack(kpp), jnp.stack(ssp), jnp.stack(cvp),
            jnp.stack(kvs), jnp.stack(kps), jnp.stack(sss), jnp.stack(cvs))
```

```python
import functools
import math

import jax
import jax.numpy as jnp
from jax import lax
from jax.experimental import pallas as pl
from jax.experimental.pallas import tpu as pltpu

F32 = jnp.float32
BF16 = jnp.bfloat16

D_MODEL = 1024
MLA_HEADS = 16
Q_LORA = 384
KV_LORA = 256
NOPE_DIM = 64
ROPE_DIM = 32
V_DIM = 64
ROPE_THETA = 10000.0
SM_SCALE = 1.0 / math.sqrt(NOPE_DIM + ROPE_DIM)
PAGE_SIZE = 128
D_INNER = 2048
SSM_HEAD_DIM = 64
SSM_HEADS = D_INNER // SSM_HEAD_DIM
SSM_GROUPS = 8
D_STATE = 128
CONV_W = 4
CONV_DIM = D_INNER + 2 * SSM_GROUPS * D_STATE
SSD_CHUNK = 128
D_FF = 4096
EPS = 1e-6

LANES = 128
SUBLANES = 8
HEAD_PAD = LANES
VMEM_LIMIT = 56 * 1024 * 1024

_NT = (((1,), (1,)), ((), ()))
_TN = (((0,), (0,)), ((), ()))


def _rms_scale(x):
    return x * lax.rsqrt(jnp.mean(x * x, axis=-1, keepdims=True) + EPS)


def _sigmoid(x):
    return 1.0 / (1.0 + jnp.exp(-x))


def _silu(x):
    return x * _sigmoid(x)


def _softplus(x):
    return jnp.maximum(x, 0.0) + jnp.log1p(jnp.exp(-jnp.abs(x)))


def _dot(a, b):
    return jnp.dot(a, b, preferred_element_type=F32)


def _const_spec(shape):
    nd = len(shape)
    return pl.BlockSpec(shape, lambda *_: (0,) * nd, pipeline_mode=pl.Buffered(1))


def _params(n_grid, vmem=VMEM_LIMIT):
    return pltpu.CompilerParams(dimension_semantics=("arbitrary",) * n_grid, vmem_limit_bytes=vmem)


def _rope_tile(t, c, s1, s2):
    return t * c + pltpu.roll(t, LANES - ROPE_DIM // 2, 1) * s1 + pltpu.roll(t, ROPE_DIM // 2, 1) * s2


def _mla_proj_kernel(*refs, prompt):
    (x_ref, gpre_ref, wlat_ref, gq_ref, wuq_ref, gkv_ref, rc_ref, rs1_ref, rs2_ref, wuk_ref) = refs[:10]
    if prompt:
        wuv_ref, qh_ref, ckv_ref, kpe_ref, kh_ref, v_ref = refs[10:]
    else:
        qh_ref, ckv_ref, kpe_ref, qlat_ref = refs[10:]
    h = (_rms_scale(x_ref[...]) * gpre_ref[...]).astype(BF16)
    lat = _dot(h, wlat_ref[...])
    cq = (_rms_scale(lat[:, :Q_LORA]) * gq_ref[...]).astype(BF16)
    ckv = _rms_scale(lat[:, Q_LORA:Q_LORA + KV_LORA]) * gkv_ref[...]
    ckv_ref[...] = ckv
    c, s1, s2 = rc_ref[...], rs1_ref[...], rs2_ref[...]
    kpe = _rope_tile(lat[:, Q_LORA + KV_LORA:], c, s1, s2)
    kpe_ref[...] = kpe[:, NOPE_DIM:NOPE_DIM + ROPE_DIM]
    q = _dot(cq, wuq_ref[...])
    ckv_b = ckv.astype(BF16)
    if prompt:
        k = _dot(ckv_b, wuk_ref[...])
        v_ref[...] = _dot(ckv_b, wuv_ref[...]).astype(BF16)
    for hd in range(MLA_HEADS):
        sl = slice(hd * HEAD_PAD, (hd + 1) * HEAD_PAD)
        qh = _rope_tile(q[:, sl], c, s1, s2).astype(BF16)
        qh_ref[:, sl] = qh
        if prompt:
            kh_ref[:, sl] = (k[:, sl] + kpe).astype(BF16)
        else:
            qlat_ref[:, hd * KV_LORA:(hd + 1) * KV_LORA] = lax.dot_general(
                qh, wuk_ref[:, sl], _NT, preferred_element_type=F32)


def _mla_proj(x, gpre, wlat, gq, wuq, gkv, rope_tabs, wuk, wuv, *, prompt, tm, tab_blocks):
    n = x.shape[0]
    hp = MLA_HEADS * HEAD_PAD
    row = lambda w: pl.BlockSpec((tm, w), lambda i: (i, 0))
    tab = pl.BlockSpec((tm, LANES), lambda i: (i % tab_blocks, 0))
    in_specs = [row(D_MODEL), _const_spec(gpre.shape), _const_spec(wlat.shape), _const_spec(gq.shape),
                _const_spec(wuq.shape), _const_spec(gkv.shape), tab, tab, tab, _const_spec(wuk.shape)]
    args = [x, gpre, wlat, gq, wuq, gkv, *rope_tabs, wuk]
    out_shape = [jax.ShapeDtypeStruct((n, hp), BF16), jax.ShapeDtypeStruct((n, KV_LORA), F32),
                 jax.ShapeDtypeStruct((n, ROPE_DIM), F32)]
    out_specs = [row(hp), row(KV_LORA), row(ROPE_DIM)]
    if prompt:
        in_specs.append(_const_spec(wuv.shape))
        args.append(wuv)
        out_shape += [jax.ShapeDtypeStruct((n, hp), BF16), jax.ShapeDtypeStruct((n, MLA_HEADS * V_DIM), BF16)]
        out_specs += [row(hp), row(MLA_HEADS * V_DIM)]
    else:
        out_shape.append(jax.ShapeDtypeStruct((n, MLA_HEADS * KV_LORA), F32))
        out_specs.append(row(MLA_HEADS * KV_LORA))
    return pl.pallas_call(
        functools.partial(_mla_proj_kernel, prompt=prompt),
        grid=(n // tm,), in_specs=in_specs, out_specs=out_specs, out_shape=out_shape,
        compiler_params=_params(1), name="mla_proj_prompt" if prompt else "mla_proj_sample")(*args)


_SSM_OFF_Z = 0
_SSM_OFF_XBC = D_INNER
_SSM_OFF_DT = D_INNER + CONV_DIM
_SSM_OFF_GA = _SSM_OFF_DT + LANES
_SSM_OFF_GB = _SSM_OFF_GA + D_MODEL
_SSM_COLS = _SSM_OFF_GB + D_MODEL


def _ssm_proj_kernel(x_ref, gpre_ref, w_ref, z_ref, xbc_ref, dt_ref, sga_ref, sgb_ref):
    h = (_rms_scale(x_ref[...]) * gpre_ref[...]).astype(BF16)
    z_ref[...] = _dot(h, w_ref[:, _SSM_OFF_Z:_SSM_OFF_XBC])
    xbc_ref[...] = _dot(h, w_ref[:, _SSM_OFF_XBC:_SSM_OFF_DT])
    dt_ref[...] = _dot(h, w_ref[:, _SSM_OFF_DT:_SSM_OFF_GA])
    sga_ref[...] = _sigmoid(_dot(h, w_ref[:, _SSM_OFF_GA:_SSM_OFF_GB]))
    sgb_ref[...] = _sigmoid(_dot(h, w_ref[:, _SSM_OFF_GB:_SSM_COLS]))


def _ssm_proj(x, gpre, wssm, *, tm, name):
    n = x.shape[0]
    row = lambda w: pl.BlockSpec((tm, w), lambda i: (i, 0))
    widths = (D_INNER, CONV_DIM, LANES, D_MODEL, D_MODEL)
    return pl.pallas_call(
        _ssm_proj_kernel, grid=(n // tm,),
        in_specs=[row(D_MODEL), _const_spec(gpre.shape), _const_spec(wssm.shape)],
        out_specs=[row(w) for w in widths],
        out_shape=[jax.ShapeDtypeStruct((n, w), F32) for w in widths],
        compiler_params=_params(1), name=name)(x, gpre, wssm)


def _attn_prompt_kernel(q_ref, k_ref, v_ref, o_ref, *, seq, tq):
    lane = lax.broadcasted_iota(jnp.int32, (tq, LANES), 1)
    row = lax.broadcasted_iota(jnp.int32, (tq, tq), 0)
    col = lax.broadcasted_iota(jnp.int32, (tq, tq), 1)
    causal = col <= row
    for qi in range(seq // tq):
        lo, hi = qi * tq, (qi + 1) * tq
        outs = []
        for j in range(2):
            hs = slice(j * HEAD_PAD, (j + 1) * HEAD_PAD)
            q = q_ref[0, lo:hi, hs]
            sd = lax.dot_general(q, k_ref[0, lo:hi, hs], _NT, preferred_element_type=F32) * SM_SCALE
            sd = jnp.where(causal, sd, -jnp.inf)
            m = jnp.max(sd, axis=-1, keepdims=True)
            if qi > 0:
                sp = lax.dot_general(q, k_ref[0, :lo, hs], _NT, preferred_element_type=F32) * SM_SCALE
                m = jnp.maximum(m, jnp.max(sp, axis=-1, keepdims=True))
                pp = jnp.exp(sp - m)
                den = jnp.sum(pp, axis=-1, keepdims=True)
                acc = _dot(pp.astype(BF16), v_ref[0, :lo, :])
            pd = jnp.exp(sd - m)
            if qi > 0:
                den = den + jnp.sum(pd, axis=-1, keepdims=True)
                acc = acc + _dot(pd.astype(BF16), v_ref[0, lo:hi, :])
            else:
                den = jnp.sum(pd, axis=-1, keepdims=True)
                acc = _dot(pd.astype(BF16), v_ref[0, lo:hi, :])
            outs.append(acc / den)
        o_ref[0, lo:hi, :] = jnp.where(lane < V_DIM, outs[0], outs[1]).astype(BF16)


def _attn_prompt(qh, kh, v, *, tq=256):
    b, s, _ = qh.shape
    pair = 2 * HEAD_PAD
    return pl.pallas_call(
        functools.partial(_attn_prompt_kernel, seq=s, tq=tq),
        grid=(b, MLA_HEADS // 2),
        in_specs=[pl.BlockSpec((1, s, pair), lambda i, j: (i, 0, j)),
                  pl.BlockSpec((1, s, pair), lambda i, j: (i, 0, j)),
                  pl.BlockSpec((1, s, 2 * V_DIM), lambda i, j: (i, 0, j))],
        out_specs=pl.BlockSpec((1, s, 2 * V_DIM), lambda i, j: (i, 0, j)),
        out_shape=jax.ShapeDtypeStruct((b, s, MLA_HEADS * V_DIM), BF16),
        compiler_params=_params(2), name="attn_prompt")(qh, kh, v)


def _attn_sample_kernel(pt_ref, qlat_ref, qh_ref, ckv_ref, kpe_ref, ckv_hbm, cpe_hbm, o_ref,
                        kv_buf, pe_buf, kvb, peb, s_scr, sem, *, n_pages, chunk):
    b = pl.program_id(0)
    nb = pl.num_programs(0)
    slot = b % 2
    past = n_pages * PAGE_SIZE

    def page_copies(bb, sl, j):
        pg = pt_ref[bb, j]
        off = pl.multiple_of(j * PAGE_SIZE, PAGE_SIZE)
        return (pltpu.make_async_copy(ckv_hbm.at[pg], kv_buf.at[sl, pl.ds(off, PAGE_SIZE)], sem.at[0, sl]),
                pltpu.make_async_copy(cpe_hbm.at[pg], pe_buf.at[sl, pl.ds(off, PAGE_SIZE)], sem.at[1, sl]))

    def start_fetch(bb, sl):
        def body(j, carry):
            for cp in page_copies(bb, sl, j):
                cp.start()
            return carry
        lax.fori_loop(0, n_pages, body, 0)

    @pl.when(b == 0)
    def _():
        start_fetch(0, 0)

    @pl.when(b + 1 < nb)
    def _():
        start_fetch(b + 1, 1 - slot)

    def wait_body(j, carry):
        for cp in page_copies(b, slot, j):
            cp.wait()
        return carry
    lax.fori_loop(0, n_pages, wait_body, 0)

    qlat = qlat_ref[0]
    qlat_b = qlat.astype(BF16)
    qpe_f = qh_ref[0][:, NOPE_DIM:NOPE_DIM + ROPE_DIM].astype(F32)
    qpe_b = qpe_f.astype(BF16)
    ckv_new = ckv_ref[0]
    kpe_new = kpe_ref[0]

    m = None
    for c in range(past // chunk):
        cs = slice(c * chunk, (c + 1) * chunk)
        kc = kv_buf[slot, cs, :].astype(BF16)
        pc = pe_buf[slot, cs, :].astype(BF16)
        kvb[cs, :] = kc
        sc = (lax.dot_general(qlat_b, kc, _NT, preferred_element_type=F32)
              + lax.dot_general(qpe_b, pc, _NT, preferred_element_type=F32)) * SM_SCALE
        s_scr[:, cs] = sc
        mc = jnp.max(sc, axis=-1, keepdims=True)
        m = mc if m is None else jnp.maximum(m, mc)
    s_new = (jnp.sum(qlat * ckv_new, axis=-1, keepdims=True)
             + jnp.sum(qpe_f * kpe_new, axis=-1, keepdims=True)) * SM_SCALE
    m = jnp.maximum(m, s_new)
    p_new = jnp.exp(s_new - m)
    den = p_new
    acc = p_new * ckv_new
    for c in range(past // chunk):
        cs = slice(c * chunk, (c + 1) * chunk)
        p = jnp.exp(s_scr[:, cs] - m)
        den = den + jnp.sum(p, axis=-1, keepdims=True)
        acc = acc + _dot(p.astype(BF16), kvb[cs, :])
    o_ref[0] = acc / den


def _attn_sample(page_table, qlat, qh, ckv, kpe, cache_kv, cache_pe, *, chunk=2048):
    nb, n_pages = page_table.shape
    past = n_pages * PAGE_SIZE
    chunk = min(chunk, past)
    grid_spec = pltpu.PrefetchScalarGridSpec(
        num_scalar_prefetch=1, grid=(nb,),
        in_specs=[pl.BlockSpec((1, MLA_HEADS, KV_LORA), lambda i, pt: (i, 0, 0)),
                  pl.BlockSpec((1, MLA_HEADS, HEAD_PAD), lambda i, pt: (i, 0, 0)),
                  pl.BlockSpec((1, 1, KV_LORA), lambda i, pt: (i, 0, 0)),
                  pl.BlockSpec((1, 1, ROPE_DIM), lambda i, pt: (i, 0, 0)),
                  pl.BlockSpec(memory_space=pl.ANY),
                  pl.BlockSpec(memory_space=pl.ANY)],
        out_specs=pl.BlockSpec((1, MLA_HEADS, KV_LORA), lambda i, pt: (i, 0, 0)),
        scratch_shapes=[pltpu.VMEM((2, past, KV_LORA), F32),
                        pltpu.VMEM((2, past, ROPE_DIM), F32),
                        pltpu.VMEM((past, KV_LORA), BF16),
                        pltpu.VMEM((past, ROPE_DIM), BF16),
                        pltpu.VMEM((MLA_HEADS, past), F32),
                        pltpu.SemaphoreType.DMA((2, 2))])
    return pl.pallas_call(
        functools.partial(_attn_sample_kernel, n_pages=n_pages, chunk=chunk),
        grid_spec=grid_spec,
        out_shape=jax.ShapeDtypeStruct((nb, MLA_HEADS, KV_LORA), F32),
        compiler_params=_params(1), name="attn_sample")(
            page_table, qlat.reshape(nb, MLA_HEADS, KV_LORA), qh.reshape(nb, MLA_HEADS, HEAD_PAD),
            ckv.reshape(nb, 1, KV_LORA), kpe.reshape(nb, 1, ROPE_DIM), cache_kv, cache_pe)


def _uv_kernel(ol_ref, wuv_ref, o_ref):
    lane = lax.broadcasted_iota(jnp.int32, o_ref.shape[:1] + (LANES,), 1)
    for hp in range(MLA_HEADS // 2):
        w = wuv_ref[:, hp * LANES:(hp + 1) * LANES]
        a0 = _dot(ol_ref[:, (2 * hp) * KV_LORA:(2 * hp + 1) * KV_LORA].astype(BF16), w)
        a1 = _dot(ol_ref[:, (2 * hp + 1) * KV_LORA:(2 * hp + 2) * KV_LORA].astype(BF16), w)
        o_ref[:, hp * LANES:(hp + 1) * LANES] = jnp.where(lane < V_DIM, a0, a1).astype(BF16)


def _uv_project(o_lat, wuv):
    n = o_lat.shape[0]
    return pl.pallas_call(
        _uv_kernel, grid=(1,),
        in_specs=[_const_spec(o_lat.shape), _const_spec(wuv.shape)],
        out_specs=pl.BlockSpec((n, MLA_HEADS * V_DIM), lambda i: (0, 0)),
        out_shape=jax.ShapeDtypeStruct((n, MLA_HEADS * V_DIM), BF16),
        compiler_params=_params(1), name="uv_project")(o_lat, wuv)


def _cumsum_rows(v):
    rowi = lax.broadcasted_iota(jnp.int32, v.shape, 0)
    k = 1
    while k < v.shape[0]:
        v = v + jnp.where(rowi >= k, pltpu.roll(v, k, 0), 0.0)
        k *= 2
    return v


def _pair_cols(mat, h0, lane):
    t = mat.shape[0]
    a = jnp.broadcast_to(mat[:, h0:h0 + 1], (t, LANES))
    b = jnp.broadcast_to(mat[:, h0 + 1:h0 + 2], (t, LANES))
    return jnp.where(lane < SSM_HEAD_DIM, a, b)


def _ssd_prompt_kernel(xbc_ref, z_ref, dt_ref, wconv_ref, bconv_ref, dtb_ref, alog_ref, dskip_ref, gssm_ref,
                       y_ref, state_ref, conv_ref, xpad, state):
    t = SSD_CHUNK
    c = pl.program_id(1)
    tail = CONV_W - 1

    @pl.when(c == 0)
    def _():
        xpad[0:SUBLANES, :] = jnp.zeros((SUBLANES, CONV_DIM), F32)
        state[...] = jnp.zeros(state.shape, F32)

    @pl.when(c > 0)
    def _():
        xpad[0:SUBLANES, :] = xpad[t:t + SUBLANES, :]

    xpad[SUBLANES:SUBLANES + t, :] = xbc_ref[...]
    conv_ref[0] = xpad[SUBLANES + t - tail:SUBLANES + t, :]

    def conv(lo, width):
        acc = bconv_ref[:, lo:lo + width]
        for k in range(CONV_W):
            r0 = SUBLANES - tail + k
            acc = acc + xpad[r0:r0 + t, lo:lo + width] * wconv_ref[k:k + 1, lo:lo + width]
        return _silu(acc)

    dt = _softplus(dt_ref[...] + dtb_ref[...])
    a = -jnp.exp(alog_ref[...])
    cs = _cumsum_rows(dt * a)
    cs_last = cs[t - 1:t, :]
    ecs = jnp.exp(cs)
    wloc = dt * jnp.exp(cs_last - cs)
    cs_t = cs.T
    dt_t = dt.T
    cdec = jnp.broadcast_to(jnp.exp(cs_t[:, t - 1:t]), (LANES, LANES))

    lane = lax.broadcasted_iota(jnp.int32, (t, LANES), 1)
    tri = lax.broadcasted_iota(jnp.int32, (t, t), 0) >= lax.broadcasted_iota(jnp.int32, (t, t), 1)
    hpg = SSM_HEADS // SSM_GROUPS
    gw = hpg * SSM_HEAD_DIM

    for g in range(SSM_GROUPS):
        bm_b = conv(D_INNER + g * D_STATE, D_STATE).astype(BF16)
        cm_b = conv(D_INNER + SSM_GROUPS * D_STATE + g * D_STATE, D_STATE).astype(BF16)
        cb = lax.dot_general(cm_b, bm_b, _NT, preferred_element_type=F32)
        prev_b = state[g * gw:(g + 1) * gw, :].astype(BF16)
        yoff = lax.dot_general(cm_b, prev_b, _NT, preferred_element_type=F32)
        ys = []
        for pr in range(hpg // 2):
            h0 = g * hpg + 2 * pr
            x2 = conv(h0 * SSM_HEAD_DIM, LANES)
            x2_b = x2.astype(BF16)
            yd = []
            for h in (h0, h0 + 1):
                seg = jnp.broadcast_to(cs[:, h:h + 1], (t, t)) - cs_t[h:h + 1, :]
                dec = jnp.exp(jnp.where(tri, seg, -jnp.inf))
                mh = (cb * dec * dt_t[h:h + 1, :]).astype(BF16)
                yd.append(_dot(mh, x2_b))
            y2 = jnp.where(lane < SSM_HEAD_DIM, yd[0], yd[1])
            y2 = y2 + yoff[:, 2 * pr * SSM_HEAD_DIM:(2 * pr + 2) * SSM_HEAD_DIM] * _pair_cols(ecs, h0, lane)
            y2 = y2 + dskip_ref[:, h0 * SSM_HEAD_DIM:(h0 + 2) * SSM_HEAD_DIM] * x2
            ys.append(y2)
            xw = (x2 * _pair_cols(wloc, h0, lane)).astype(BF16)
            loc = lax.dot_general(xw, bm_b, _TN, preferred_element_type=F32)
            r0 = h0 * SSM_HEAD_DIM
            for k, h in enumerate((h0, h0 + 1)):
                rs = slice(r0 + k * SSM_HEAD_DIM, r0 + (k + 1) * SSM_HEAD_DIM)
                state[rs, :] = (state[rs, :] * cdec[h:h + 1, :]
                                + loc[k * SSM_HEAD_DIM:(k + 1) * SSM_HEAD_DIM, :])
        yg = jnp.concatenate(ys, axis=1) * _silu(z_ref[:, g * gw:(g + 1) * gw])
        y_ref[:, g * gw:(g + 1) * gw] = (_rms_scale(yg) * gssm_ref[:, g * gw:(g + 1) * gw]).astype(BF16)

    state_ref[0] = state[...]


def _ssd_prompt(xbc, z, dt, wconv, bconv, dtb, alog, dskip, gssm, *, batch, seq):
    nc = seq // SSD_CHUNK
    row = lambda w: pl.BlockSpec((SSD_CHUNK, w), lambda b, c: (b * nc + c, 0))
    return pl.pallas_call(
        _ssd_prompt_kernel, grid=(batch, nc),
        in_specs=[row(CONV_DIM), row(D_INNER), row(LANES)]
        + [_const_spec(a.shape) for a in (wconv, bconv, dtb, alog, dskip, gssm)],
        out_specs=[row(D_INNER),
                   pl.BlockSpec((1, D_INNER, D_STATE), lambda b, c: (b, 0, 0)),
                   pl.BlockSpec((1, CONV_W - 1, CONV_DIM), lambda b, c: (b, 0, 0))],
        out_shape=[jax.ShapeDtypeStruct((batch * seq, D_INNER), BF16),
                   jax.ShapeDtypeStruct((batch, D_INNER, D_STATE), F32),
                   jax.ShapeDtypeStruct((batch, CONV_W - 1, CONV_DIM), F32)],
        scratch_shapes=[pltpu.VMEM((SUBLANES + SSD_CHUNK, CONV_DIM), F32),
                        pltpu.VMEM((D_INNER, D_STATE), F32)],
        compiler_params=_params(2), name="ssd_prompt")(xbc, z, dt, wconv, bconv, dtb, alog, dskip, gssm)


def _ssd_sample_pre_kernel(xbc_ref, cprev_ref, dt_ref, wconv_ref, bconv_ref, dtb_ref, alog_ref,
                           cnew_ref, xs_ref, xdt_t_ref, bm_ref, cm_ref, dec_ref):
    tail = CONV_W - 1
    acc = bconv_ref[...]
    for k in range(tail):
        acc = acc + cprev_ref[:, k * CONV_DIM:(k + 1) * CONV_DIM] * wconv_ref[k:k + 1, :]
    x_new = xbc_ref[...]
    acc = acc + x_new * wconv_ref[tail:tail + 1, :]
    u = _silu(acc)
    for k in range(1, tail):
        cnew_ref[:, (k - 1) * CONV_DIM:k * CONV_DIM] = cprev_ref[:, k * CONV_DIM:(k + 1) * CONV_DIM]
    cnew_ref[:, (tail - 1) * CONV_DIM:] = x_new
    xs = u[:, :D_INNER]
    xs_ref[...] = xs
    bm_ref[...] = u[:, D_INNER:D_INNER + SSM_GROUPS * D_STATE]
    cm_ref[...] = u[:, D_INNER + SSM_GROUPS * D_STATE:]
    dt = _softplus(dt_ref[...] + dtb_ref[...])
    dec_ref[...] = jnp.exp(dt * (-jnp.exp(alog_ref[...])))
    nb = xs.shape[0]
    lane = lax.broadcasted_iota(jnp.int32, (nb, LANES), 1)
    for pr in range(SSM_HEADS // 2):
        x2 = xs[:, pr * LANES:(pr + 1) * LANES] * _pair_cols(dt, 2 * pr, lane)
        xdt_t_ref[pr * LANES:(pr + 1) * LANES, :] = x2.T


def _ssd_sample_pre(xbc, cprev_flat, dt, wconv, bconv, dtb, alog):
    nb = xbc.shape[0]
    ins = (xbc, cprev_flat, dt, wconv, bconv, dtb, alog)
    shapes = [((nb, (CONV_W - 1) * CONV_DIM), F32), ((nb, D_INNER), F32), ((D_INNER, nb), F32),
              ((nb, SSM_GROUPS * D_STATE), F32), ((nb, SSM_GROUPS * D_STATE), F32), ((nb, LANES), F32)]
    return pl.pallas_call(
        _ssd_sample_pre_kernel, grid=(1,),
        in_specs=[_const_spec(a.shape) for a in ins],
        out_specs=[pl.BlockSpec(s, lambda i: (0, 0)) for s, _ in shapes],
        out_shape=[jax.ShapeDtypeStruct(s, d) for s, d in shapes],
        compiler_params=_params(1), name="ssd_sample_pre")(*ins)


def _ssd_sample_state_kernel(dec_ref, st_ref, xdt_t_ref, bm_ref, cm_ref, xs_ref, z_ref, dskip_ref, gssm_ref,
                             st_out_ref, y_ref, y_t):
    b = pl.program_id(0)
    nb = pl.num_programs(0)
    hpg = SSM_HEADS // SSM_GROUPS
    gw = hpg * SSM_HEAD_DIM
    lane = lax.broadcasted_iota(jnp.int32, (gw, nb), 1)
    is_b = lane == b
    for g in range(SSM_GROUPS):
        rows = slice(g * gw, (g + 1) * gw)
        x_col = jnp.sum(jnp.where(is_b, xdt_t_ref[rows, :], 0.0), axis=1, keepdims=True)
        bm_row = bm_ref[0, g:g + 1, :]
        cm_row = cm_ref[0, g:g + 1, :]
        pieces = []
        for hh in range(hpg):
            h = g * hpg + hh
            rs = slice(h * SSM_HEAD_DIM, (h + 1) * SSM_HEAD_DIM)
            pieces.append(st_ref[0, rs, :] * dec_ref[b, h])
        new = jnp.concatenate(pieces, axis=0) + x_col * bm_row
        st_out_ref[0, rows, :] = new
        y_col = jnp.sum(new * cm_row, axis=1, keepdims=True)
        y_t[rows, :] = jnp.where(is_b, y_col, y_t[rows, :])

    @pl.when(b == nb - 1)
    def _():
        for g in range(SSM_GROUPS):
            cols = slice(g * gw, (g + 1) * gw)
            y = y_t[cols, :].T + dskip_ref[:, cols] * xs_ref[:, cols]
            y = y * _silu(z_ref[:, cols])
            y_ref[:, cols] = (_rms_scale(y) * gssm_ref[:, cols]).astype(BF16)


def _ssd_sample_state(dec, state, xdt_t, bm, cm, xs, z, dskip, gssm):
    nb = state.shape[0]
    grid_spec = pl.GridSpec(
        grid=(nb,),
        in_specs=[pl.BlockSpec(memory_space=pltpu.SMEM),
                  pl.BlockSpec((1, D_INNER, D_STATE), lambda i: (i, 0, 0)),
                  _const_spec(xdt_t.shape),
                  pl.BlockSpec((1, SSM_GROUPS, D_STATE), lambda i: (i, 0, 0)),
                  pl.BlockSpec((1, SSM_GROUPS, D_STATE), lambda i: (i, 0, 0)),
                  _const_spec(xs.shape), _const_spec(z.shape), _const_spec(dskip.shape), _const_spec(gssm.shape)],
        out_specs=[pl.BlockSpec((1, D_INNER, D_STATE), lambda i: (i, 0, 0)),
                   pl.BlockSpec((nb, D_INNER), lambda i: (0, 0))],
        scratch_shapes=[pltpu.VMEM((D_INNER, nb), F32)])
    return pl.pallas_call(
        _ssd_sample_state_kernel, grid_spec=grid_spec,
        out_shape=[jax.ShapeDtypeStruct(state.shape, F32), jax.ShapeDtypeStruct((nb, D_INNER), BF16)],
        compiler_params=_params(1), name="ssd_sample_state")(
            dec, state, xdt_t, bm.reshape(nb, SSM_GROUPS, D_STATE), cm.reshape(nb, SSM_GROUPS, D_STATE),
            xs, z, dskip, gssm)


def _post_kernel(x_ref, oa_ref, os_ref, sga_ref, sgb_ref, wa_ref, ws_ref, wo_ref, gpm_ref, gpre_ref,
                 wup_ref, wdn_ref, gpost_ref, y_ref, *, ff_chunk):
    merged = sga_ref[...] * _dot(oa_ref[...], wa_ref[...]) + sgb_ref[...] * _dot(os_ref[...], ws_ref[...])
    mix = _dot(merged.astype(BF16), wo_ref[...])
    x1 = x_ref[...] + _rms_scale(mix) * gpm_ref[...]
    h2 = (_rms_scale(x1) * gpre_ref[...]).astype(BF16)
    f = None
    for c in range(D_FF // ff_chunk):
        cs = slice(c * ff_chunk, (c + 1) * ff_chunk)
        u = jnp.square(jnp.maximum(_dot(h2, wup_ref[:, cs]), 0.0)).astype(BF16)
        d = _dot(u, wdn_ref[cs, :])
        f = d if f is None else f + d
    y_ref[...] = x1 + _rms_scale(f) * gpost_ref[...]


def _post(x, oa, os_, sga, sgb, wa, ws, wo, gpm, gpre, wup, wdn, gpost, *, tm, name, ff_chunk=1024):
    n = x.shape[0]
    row = lambda w: pl.BlockSpec((tm, w), lambda i: (i, 0))
    consts = (wa, ws, wo, gpm, gpre, wup, wdn, gpost)
    return pl.pallas_call(
        functools.partial(_post_kernel, ff_chunk=ff_chunk), grid=(n // tm,),
        in_specs=[row(D_MODEL), row(MLA_HEADS * V_DIM), row(D_INNER), row(D_MODEL), row(D_MODEL)]
        + [_const_spec(a.shape) for a in consts],
        out_specs=row(D_MODEL),
        out_shape=jax.ShapeDtypeStruct((n, D_MODEL), F32),
        compiler_params=_params(1), name=name)(x, oa, os_, sga, sgb, *consts)


def _rope_tables(pos):
    half = ROPE_DIM // 2
    inv_freq = ROPE_THETA ** (-jnp.arange(half, dtype=F32) / half)
    ang = pos.astype(F32)[:, None] * inv_freq[None, :]
    cos, sin = jnp.cos(ang), jnp.sin(ang)
    n = pos.shape[0]
    ones_lo = jnp.ones((n, NOPE_DIM), F32)
    pad_hi = jnp.ones((n, HEAD_PAD - NOPE_DIM - ROPE_DIM), F32)
    zeros = lambda w: jnp.zeros((n, w), F32)
    c = jnp.concatenate([ones_lo, cos, cos, pad_hi], axis=1)
    s1 = jnp.concatenate([zeros(NOPE_DIM), -sin, zeros(HEAD_PAD - NOPE_DIM - half)], axis=1)
    s2 = jnp.concatenate([zeros(NOPE_DIM + half), sin, zeros(HEAD_PAD - NOPE_DIM - ROPE_DIM)], axis=1)
    return c, s1, s2


def _pad_cols(w, lo, total):
    return jnp.pad(w, ((0, 0), (lo, total - lo - w.shape[1])))


def _row(v):
    return v.reshape(1, -1)


def kernel(x_prompt, x_sample, cache_kv_latent, cache_k_rope, state_ssm, state_conv, page_table, w_in, g_pre_mix, g_q, w_uq, g_kv, w_uk, w_uv, w_conv, b_conv, dt_bias, a_log, d_skip, g_ssm, w_br_attn, w_br_ssm, w_out, g_post_mix, g_pre_mlp, w_up, w_down, g_post_mlp):
    depth = w_in.shape[0]
    assert depth == 1
    bp, sp, _ = x_prompt.shape
    bs, ss, _ = x_sample.shape
    assert ss == 1
    past = page_table.shape[1] * PAGE_SIZE
    l = 0

    o_q, o_kv, o_pe = 0, Q_LORA, Q_LORA + KV_LORA
    o_z = o_pe + ROPE_DIM
    o_xbc = o_z + D_INNER
    o_dt = o_xbc + CONV_DIM
    o_ga = o_dt + SSM_HEADS
    o_gb = o_ga + D_MODEL
    wi = w_in[l]
    wlat = jnp.concatenate([wi[:, o_q:o_pe], _pad_cols(wi[:, o_pe:o_z], NOPE_DIM, HEAD_PAD)], axis=1).astype(BF16)
    wssm = jnp.concatenate([wi[:, o_z:o_dt], _pad_cols(wi[:, o_dt:o_ga], 0, LANES), wi[:, o_ga:]], axis=1).astype(BF16)
    wuq = jnp.pad(w_uq[l], ((0, 0), (0, 0), (0, HEAD_PAD - NOPE_DIM - ROPE_DIM))).reshape(Q_LORA, -1).astype(BF16)
    wuk = jnp.pad(w_uk[l], ((0, 0), (0, 0), (0, HEAD_PAD - NOPE_DIM))).reshape(KV_LORA, -1).astype(BF16)
    wuv = w_uv[l].reshape(KV_LORA, -1).astype(BF16)
    gpre, gq, gkv = _row(g_pre_mix[l]), _row(g_q[l]), _row(g_kv[l])
    wconv, bconv = w_conv[l], _row(b_conv[l])
    dtb = _pad_cols(_row(dt_bias[l]), 0, LANES)
    alog = _pad_cols(_row(a_log[l]), 0, LANES)
    dskip = _row(jnp.repeat(d_skip[l], SSM_HEAD_DIM))
    gssm = _row(g_ssm[l])
    post_w = (w_br_attn[l].astype(BF16), w_br_ssm[l].astype(BF16), w_out[l].astype(BF16), _row(g_post_mix[l]),
              _row(g_pre_mlp[l]), w_up[l].astype(BF16), w_down[l].astype(BF16), _row(g_post_mlp[l]))

    xp = x_prompt.reshape(bp * sp, D_MODEL)
    tm_p = min(512, sp)
    tabs_p = _rope_tables(jnp.arange(sp, dtype=jnp.int32))
    qh_p, ckv_p, kpe_p, kh_p, v_p = _mla_proj(xp, gpre, wlat, gq, wuq, gkv, tabs_p, wuk, wuv,
                                              prompt=True, tm=tm_p, tab_blocks=sp // tm_p)
    z_p, xbc_p, dt_p, sga_p, sgb_p = _ssm_proj(xp, gpre, wssm, tm=min(256, sp), name="ssm_proj_prompt")
    oa_p = _attn_prompt(qh_p.reshape(bp, sp, -1), kh_p.reshape(bp, sp, -1), v_p.reshape(bp, sp, -1),
                        tq=min(256, sp))
    os_p, ssm_p, conv_p = _ssd_prompt(xbc_p, z_p, dt_p, wconv, bconv, dtb, alog, dskip, gssm, batch=bp, seq=sp)
    y_p = _post(xp, oa_p.reshape(bp * sp, -1), os_p, sga_p, sgb_p, *post_w, tm=min(256, sp), name="post_prompt")

    xs_in = x_sample.reshape(bs, D_MODEL)
    tabs_s = _rope_tables(jnp.full((bs,), past, dtype=jnp.int32))
    qh_s, ckv_s, kpe_s, qlat_s = _mla_proj(xs_in, gpre, wlat, gq, wuq, gkv, tabs_s, wuk, None,
                                           prompt=False, tm=bs, tab_blocks=1)
    z_s, xbc_s, dt_s, sga_s, sgb_s = _ssm_proj(xs_in, gpre, wssm, tm=bs, name="ssm_proj_sample")
    o_lat = _attn_sample(page_table, qlat_s, qh_s, ckv_s, kpe_s, cache_kv_latent[l], cache_k_rope[l])
    oa_s = _uv_project(o_lat.reshape(bs, MLA_HEADS * KV_LORA), wuv)
    cnew_s, xs_s, xdt_t, bm_s, cm_s, dec_s = _ssd_sample_pre(
        xbc_s, state_conv[l].reshape(bs, (CONV_W - 1) * CONV_DIM), dt_s, wconv, bconv, dtb, alog)
    ssm_s, os_s = _ssd_sample_state(dec_s[:, :SSM_HEADS], state_ssm[l].reshape(bs, D_INNER, D_STATE),
                                    xdt_t, bm_s, cm_s, xs_s, z_s, dskip, gssm)
    y_s = _post(xs_in, oa_s, os_s, sga_s, sgb_s, *post_w, tm=bs, name="post_sample")

    hshape = (SSM_HEADS, SSM_HEAD_DIM, D_STATE)
    return (y_p.reshape(bp, sp, D_MODEL), y_s.reshape(bs, 1, D_MODEL),
            ckv_p.reshape(1, bp, sp, KV_LORA), kpe_p.reshape(1, bp, sp, ROPE_DIM),
            ssm_p.reshape(1, bp, *hshape), conv_p.reshape(1, bp, CONV_W - 1, CONV_DIM),
            ckv_s.reshape(1, bs, 1, KV_LORA), kpe_s.reshape(1, bs, 1, ROPE_DIM),
            ssm_s.reshape(1, bs, *hshape), cnew_s.reshape(1, bs, CONV_W - 1, CONV_DIM))
```

```python
import functools
import math

import jax
import jax.numpy as jnp
from jax import lax
from jax.experimental import pallas as pl
from jax.experimental.pallas import tpu as pltpu

F32 = jnp.float32
BF16 = jnp.bfloat16

D_MODEL = 1024
MLA_HEADS = 16
Q_LORA = 384
KV_LORA = 256
NOPE_DIM = 64
ROPE_DIM = 32
V_DIM = 64
ROPE_THETA = 10000.0
SM_SCALE = 1.0 / math.sqrt(NOPE_DIM + ROPE_DIM)
_LOG2E = math.log2(math.e)
_EXP2_SCALE = SM_SCALE * _LOG2E
PAGE_SIZE = 128
D_INNER = 2048
SSM_HEAD_DIM = 64
SSM_HEADS = D_INNER // SSM_HEAD_DIM
SSM_GROUPS = 8
D_STATE = 128
CONV_W = 4
CONV_DIM = D_INNER + 2 * SSM_GROUPS * D_STATE
SSD_CHUNK = 128
D_FF = 4096
EPS = 1e-6

LANES = 128
SUBLANES = 8
HEAD_PAD = LANES
VMEM_LIMIT = 56 * 1024 * 1024

_NT = (((1,), (1,)), ((), ()))
_TN = (((0,), (0,)), ((), ()))


def _rms_scale(x):
    return x * lax.rsqrt(jnp.mean(x * x, axis=-1, keepdims=True) + EPS)


def _sigmoid(x):
    return 1.0 / (1.0 + jnp.exp(-x))


def _silu(x):
    return x * _sigmoid(x)


def _softplus(x):
    return jnp.maximum(x, 0.0) + jnp.log(1.0 + jnp.exp(-jnp.abs(x)))


def _dot(a, b):
    return jnp.dot(a, b, preferred_element_type=F32)


def _const_spec(shape):
    nd = len(shape)
    return pl.BlockSpec(shape, lambda *_: (0,) * nd, pipeline_mode=pl.Buffered(1))


def _params(n_grid, vmem=VMEM_LIMIT):
    return pltpu.CompilerParams(dimension_semantics=("arbitrary",) * n_grid, vmem_limit_bytes=vmem)


def _rope_tile(t, c, s1, s2):
    return t * c + pltpu.roll(t, LANES - ROPE_DIM // 2, 1) * s1 + pltpu.roll(t, ROPE_DIM // 2, 1) * s2


def _mla_proj_kernel(*refs, prompt):
    (x_ref, gpre_ref, wlat_ref, gq_ref, wuq_ref, gkv_ref, rc_ref, rs1_ref, rs2_ref, wuk_ref) = refs[:10]
    if prompt:
        wuv_ref, vone_ref, qh_ref, ckv_ref, kpe_ref, kh_ref, v_ref = refs[10:]
    else:
        qh_ref, ckv_ref, kpe_ref, qlat_ref = refs[10:]
    h = (_rms_scale(x_ref[...]) * gpre_ref[...]).astype(BF16)
    lat = _dot(h, wlat_ref[...])
    cq = (_rms_scale(lat[:, :Q_LORA]) * gq_ref[...]).astype(BF16)
    ckv = _rms_scale(lat[:, Q_LORA:Q_LORA + KV_LORA]) * gkv_ref[...]
    ckv_ref[...] = ckv
    c, s1, s2 = rc_ref[...], rs1_ref[...], rs2_ref[...]
    kpe = _rope_tile(lat[:, Q_LORA + KV_LORA:], c, s1, s2)
    kpe_ref[...] = kpe[:, NOPE_DIM:NOPE_DIM + ROPE_DIM]
    q = _dot(cq, wuq_ref[...])
    ckv_b = ckv.astype(BF16)
    if prompt:
        k = _dot(ckv_b, wuk_ref[...])
        v_ref[...] = (_dot(ckv_b, wuv_ref[...]) + vone_ref[...]).astype(BF16)
    for hd in range(MLA_HEADS):
        sl = slice(hd * HEAD_PAD, (hd + 1) * HEAD_PAD)
        qh = _rope_tile(q[:, sl], c, s1, s2)
        if prompt:
            qh = qh * _EXP2_SCALE
        qh = qh.astype(BF16)
        qh_ref[:, sl] = qh
        if prompt:
            kh_ref[:, sl] = (k[:, sl] + kpe).astype(BF16)
        else:
            qlat_ref[:, hd * KV_LORA:(hd + 1) * KV_LORA] = lax.dot_general(
                qh, wuk_ref[:, sl], _NT, preferred_element_type=F32)


def _mla_proj(x, gpre, wlat, gq, wuq, gkv, rope_tabs, wuk, wuv, *, prompt, tm, tab_blocks):
    n = x.shape[0]
    hp = MLA_HEADS * HEAD_PAD
    row = lambda w: pl.BlockSpec((tm, w), lambda i: (i, 0))
    tab = pl.BlockSpec((tm, LANES), lambda i: (i % tab_blocks, 0))
    in_specs = [row(D_MODEL), _const_spec(gpre.shape), _const_spec(wlat.shape), _const_spec(gq.shape),
                _const_spec(wuq.shape), _const_spec(gkv.shape), tab, tab, tab, _const_spec(wuk.shape)]
    args = [x, gpre, wlat, gq, wuq, gkv, *rope_tabs, wuk]
    out_shape = [jax.ShapeDtypeStruct((n, hp), BF16), jax.ShapeDtypeStruct((n, KV_LORA), F32),
                 jax.ShapeDtypeStruct((n, ROPE_DIM), F32)]
    out_specs = [row(hp), row(KV_LORA), row(ROPE_DIM)]
    if prompt:
        wuv_aug, v_ones = wuv
        in_specs += [_const_spec(wuv_aug.shape), _const_spec(v_ones.shape)]
        args += [wuv_aug, v_ones]
        out_shape += [jax.ShapeDtypeStruct((n, hp), BF16), jax.ShapeDtypeStruct((n, hp), BF16)]
        out_specs += [row(hp), row(hp)]
    else:
        out_shape.append(jax.ShapeDtypeStruct((n, MLA_HEADS * KV_LORA), F32))
        out_specs.append(row(MLA_HEADS * KV_LORA))
    return pl.pallas_call(
        functools.partial(_mla_proj_kernel, prompt=prompt),
        grid=(n // tm,), in_specs=in_specs, out_specs=out_specs, out_shape=out_shape,
        compiler_params=_params(1), name="mla_proj_prompt" if prompt else "mla_proj_sample")(*args)


_SSM_OFF_Z = 0
_SSM_OFF_XBC = D_INNER
_SSM_OFF_DT = D_INNER + CONV_DIM
_SSM_OFF_GA = _SSM_OFF_DT + LANES
_SSM_OFF_GB = _SSM_OFF_GA + D_MODEL
_SSM_COLS = _SSM_OFF_GB + D_MODEL


_CONV_TAIL = CONV_W - 1
_CONV_COLS = 512


def _ssm_proj_kernel(*refs, conv, blocks_per_seq):
    if conv:
        (x_ref, gpre_ref, w_ref, wconv_ref, bconv_ref,
         z_ref, xs_ref, bc_ref, dt_ref, sga_ref, sgb_ref, tail_ref, xpad, h_scr) = refs
    else:
        x_ref, gpre_ref, w_ref, z_ref, xbc_ref, dt_ref, sga_ref, sgb_ref, h_scr = refs
    tm = x_ref.shape[0]
    h_scr[...] = (_rms_scale(x_ref[...]) * gpre_ref[...]).astype(BF16)

    def project(off, width):
        return _dot(h_scr[...], w_ref[:, off:off + width])

    def plain(out_ref, off, width, act):
        def run(lo):
            w = min(_CONV_COLS, width - lo)
            out_ref[:, lo:lo + w] = act(project(off + lo, w))
        return [functools.partial(run, lo) for lo in range(0, width, _CONV_COLS)]

    ident = lambda v: v
    others = (plain(z_ref, _SSM_OFF_Z, D_INNER, ident) + plain(sga_ref, _SSM_OFF_GA, D_MODEL, _sigmoid)
              + plain(sgb_ref, _SSM_OFF_GB, D_MODEL, _sigmoid) + plain(dt_ref, _SSM_OFF_DT, LANES, ident))
    if not conv:
        xbc_ref[...] = project(_SSM_OFF_XBC, CONV_DIM)
        for f in others:
            f()
        return

    first = pl.program_id(0) % blocks_per_seq == 0
    n_chunks = CONV_DIM // _CONV_COLS
    n_tiles = tm // SUBLANES

    @pl.when(first)
    def _():
        xpad[:, 0:SUBLANES, :] = jnp.zeros((n_chunks, SUBLANES, _CONV_COLS), F32)

    @pl.when(jnp.logical_not(first))
    def _():
        xpad[:, 0:SUBLANES, :] = xpad[:, tm:tm + SUBLANES, :]

    sub = lax.broadcasted_iota(jnp.int32, (n_tiles, SUBLANES, _CONV_COLS), 1)

    def project_chunk(i):
        xpad[i, SUBLANES:, :] = project(_SSM_OFF_XBC + i * _CONV_COLS, _CONV_COLS)

    project_chunk(0)
    for i in range(n_chunks):
        lo = i * _CONV_COLS
        cols = slice(lo, lo + _CONV_COLS)
        if i + 1 < n_chunks:
            project_chunk(i + 1)
        if i < len(others):
            others[i]()
        tiles = xpad[i].reshape(n_tiles + 1, SUBLANES, _CONV_COLS)
        acc = bconv_ref[:, cols]
        for k in range(CONV_W):
            s = _CONV_TAIL - k
            if s == 0:
                shifted = tiles[1:]
            else:
                r = pltpu.roll(tiles, s, 1)
                shifted = jnp.where(sub >= s, r[1:], r[:-1])
            acc = acc + shifted.reshape(tm, _CONV_COLS) * wconv_ref[k:k + 1, cols]
        u = _silu(acc)
        if lo < D_INNER:
            xs_ref[:, cols] = u
        else:
            bc_ref[:, lo - D_INNER:lo - D_INNER + _CONV_COLS] = u.astype(BF16)
        tail_ref[0, :, cols] = xpad[i, SUBLANES + tm - _CONV_TAIL:SUBLANES + tm, :]
    for f in others[n_chunks:]:
        f()


def _ssm_proj(x, gpre, wssm, wconv=None, bconv=None, *, tm, seq=None, name):
    n = x.shape[0]
    conv = wconv is not None
    row = lambda w, dt=F32: (pl.BlockSpec((tm, w), lambda i: (i, 0)), jax.ShapeDtypeStruct((n, w), dt))
    in_specs = [pl.BlockSpec((tm, D_MODEL), lambda i: (i, 0)), _const_spec(gpre.shape), _const_spec(wssm.shape)]
    args = [x, gpre, wssm]
    scratch = [pltpu.VMEM((tm, D_MODEL), BF16)]
    if conv:
        bps = seq // tm
        in_specs += [_const_spec(wconv.shape), _const_spec(bconv.shape)]
        args += [wconv, bconv]
        outs = [row(D_INNER), row(D_INNER), row(CONV_DIM - D_INNER, BF16), row(LANES), row(D_MODEL), row(D_MODEL),
                (pl.BlockSpec((1, _CONV_TAIL, CONV_DIM), lambda i: (i // bps, 0, 0)),
                 jax.ShapeDtypeStruct((n // seq, _CONV_TAIL, CONV_DIM), F32))]
        scratch = [pltpu.VMEM((CONV_DIM // _CONV_COLS, SUBLANES + tm, _CONV_COLS), F32)] + scratch
    else:
        bps = 1
        outs = [row(D_INNER), row(CONV_DIM), row(LANES), row(D_MODEL), row(D_MODEL)]
    return pl.pallas_call(
        functools.partial(_ssm_proj_kernel, conv=conv, blocks_per_seq=bps), grid=(n // tm,),
        in_specs=in_specs, out_specs=[o[0] for o in outs], out_shape=[o[1] for o in outs],
        scratch_shapes=scratch, compiler_params=_params(1), name=name)(*args)


def _fold_lanes(x, op):
    out = x[:, :LANES]
    for c in range(1, x.shape[1] // LANES):
        out = op(out, x[:, c * LANES:(c + 1) * LANES])
    return out


def _attn_prompt_kernel(q_ref, k_ref, v_ref, o_ref, s_scr, p_scr, o_scr, *, seq, tq):
    lane = lax.broadcasted_iota(jnp.int32, (tq, LANES), 1)
    row = lax.broadcasted_iota(jnp.int32, (tq, tq), 0)
    col = lax.broadcasted_iota(jnp.int32, (tq, tq), 1)
    causal = col <= row
    chains = [(qi, j) for qi in range(seq // tq) for j in range(2)]
    row_max = {}

    def scores(c):
        qi, j = chains[c]
        hs = slice(j * HEAD_PAD, (j + 1) * HEAD_PAD)
        q = q_ref[0, qi * tq:(qi + 1) * tq, hs]
        mx = None
        for kt in range(qi + 1):
            ks = slice(kt * tq, (kt + 1) * tq)
            s = lax.dot_general(q, k_ref[0, ks, hs], _NT, preferred_element_type=F32)
            if kt == qi:
                s = jnp.where(causal, s, -jnp.inf)
            s_scr[c % 2, :, ks] = s
            m2 = _fold_lanes(s, jnp.maximum)
            mx = m2 if mx is None else jnp.maximum(mx, m2)
        row_max[c] = jnp.max(mx, axis=-1, keepdims=True)

    def numerators(c):
        qi, _ = chains[c]
        m = row_max.pop(c)
        for kt in range(qi + 1):
            ks = slice(kt * tq, (kt + 1) * tq)
            p_scr[c % 2, :, ks] = jnp.exp2(s_scr[c % 2, :, ks] - m).astype(BF16)

    def weighted_values(c):
        qi, j = chains[c]
        lo, hi = qi * tq, (qi + 1) * tq
        acc = _dot(p_scr[c % 2, :, :hi], v_ref[0, :hi, j * LANES:(j + 1) * LANES])
        out = acc / pltpu.roll(acc, V_DIM, 1)
        if j == 0:
            o_scr[...] = out
        else:
            o_ref[0, lo:hi, :] = jnp.where(lane < V_DIM, o_scr[...], out).astype(BF16)

    n = len(chains)
    for step in range(n + 2):
        if step < n:
            scores(step)
        if 1 <= step <= n:
            numerators(step - 1)
        if step >= 2:
            weighted_values(step - 2)


def _attn_prompt(qh, kh, v, *, tq=256):
    b, s, _ = qh.shape
    pair = 2 * HEAD_PAD
    return pl.pallas_call(
        functools.partial(_attn_prompt_kernel, seq=s, tq=tq),
        grid=(b, MLA_HEADS // 2),
        in_specs=[pl.BlockSpec((1, s, pair), lambda i, j: (i, 0, j)),
                  pl.BlockSpec((1, s, pair), lambda i, j: (i, 0, j)),
                  pl.BlockSpec((1, s, pair), lambda i, j: (i, 0, j))],
        out_specs=pl.BlockSpec((1, s, 2 * V_DIM), lambda i, j: (i, 0, j)),
        out_shape=jax.ShapeDtypeStruct((b, s, MLA_HEADS * V_DIM), BF16),
        scratch_shapes=[pltpu.VMEM((2, tq, s), F32), pltpu.VMEM((2, tq, s), BF16),
                        pltpu.VMEM((tq, LANES), F32)],
        compiler_params=_params(2), name="attn_prompt")(qh, kh, v)


def _attn_sample_kernel(pt_ref, qlat_ref, qh_ref, ckv_ref, kpe_ref, ckv_hbm, cpet_hbm, o_ref,
                        kv_buf, pe_buf, kvb, s_scr, sem, *, n_pages, tile):
    b = pl.program_id(0)
    nb = pl.num_programs(0)
    slot = b % 2
    past = n_pages * PAGE_SIZE
    n_tiles = past // tile

    def start_fetch(bb, sl):
        for j in range(n_pages):
            pg = pt_ref[bb, j]
            rows = pl.ds(j * PAGE_SIZE, PAGE_SIZE)
            pltpu.make_async_copy(ckv_hbm.at[pg], kv_buf.at[sl, rows], sem.at[0, sl]).start()
            pltpu.make_async_copy(cpet_hbm.at[pg], pe_buf.at[sl, :, rows], sem.at[1, sl]).start()

    def wait_fetch(sl):
        pltpu.make_async_copy(kv_buf.at[sl], kv_buf.at[sl], sem.at[0, sl]).wait()
        pltpu.make_async_copy(pe_buf.at[sl], pe_buf.at[sl], sem.at[1, sl]).wait()

    @pl.when(b == 0)
    def _():
        start_fetch(0, 0)

    start_fetch(jnp.minimum(b + 1, nb - 1), 1 - slot)
    wait_fetch(slot)

    qlat = qlat_ref[0]
    qlat_b = qlat.astype(BF16)
    qpe_b = qh_ref[0][:, NOPE_DIM:NOPE_DIM + ROPE_DIM]
    qpe_f = qpe_b.astype(F32)
    ckv_new = ckv_ref[0]
    kpe_new = kpe_ref[0]

    mx = None
    for t in range(n_tiles):
        ts = slice(t * tile, (t + 1) * tile)
        kc = kv_buf[slot, ts, :].astype(BF16)
        kvb[ts, :] = kc
        sc = (lax.dot_general(qlat_b, kc, _NT, preferred_element_type=F32)
              + _dot(qpe_b, pe_buf[slot, :, ts].astype(BF16)))
        s_scr[:, ts] = sc
        mx = sc if mx is None else jnp.maximum(mx, sc)
    s_new = (jnp.sum(qlat * ckv_new, axis=-1, keepdims=True)
             + jnp.sum(qpe_f * kpe_new, axis=-1, keepdims=True))
    m = jnp.maximum(jnp.max(mx, axis=-1, keepdims=True), s_new)
    p_new = jnp.exp2((s_new - m) * _EXP2_SCALE)
    psum = None
    accs = [p_new * ckv_new, None]
    for t in range(n_tiles):
        ts = slice(t * tile, (t + 1) * tile)
        p = jnp.exp2((s_scr[:, ts] - m) * _EXP2_SCALE)
        psum = p if psum is None else psum + p
        d = _dot(p.astype(BF16), kvb[ts, :])
        accs[t % 2] = d if accs[t % 2] is None else accs[t % 2] + d
    den = p_new + jnp.sum(psum, axis=-1, keepdims=True)
    acc = accs[0] if accs[1] is None else accs[0] + accs[1]
    o_ref[0] = acc / den

    @pl.when(b == nb - 1)
    def _():
        wait_fetch(1 - slot)


def _attn_sample(page_table, qlat, qh, ckv, kpe, cache_kv, cache_pe_t, *, tile=256):
    nb, n_pages = page_table.shape
    past = n_pages * PAGE_SIZE
    grid_spec = pltpu.PrefetchScalarGridSpec(
        num_scalar_prefetch=1, grid=(nb,),
        in_specs=[pl.BlockSpec((1, MLA_HEADS, KV_LORA), lambda i, pt: (i, 0, 0)),
                  pl.BlockSpec((1, MLA_HEADS, HEAD_PAD), lambda i, pt: (i, 0, 0)),
                  pl.BlockSpec((1, 1, KV_LORA), lambda i, pt: (i, 0, 0)),
                  pl.BlockSpec((1, 1, ROPE_DIM), lambda i, pt: (i, 0, 0)),
                  pl.BlockSpec(memory_space=pl.ANY),
                  pl.BlockSpec(memory_space=pl.ANY)],
        out_specs=pl.BlockSpec((1, MLA_HEADS, KV_LORA), lambda i, pt: (i, 0, 0)),
        scratch_shapes=[pltpu.VMEM((2, past, KV_LORA), F32),
                        pltpu.VMEM((2, ROPE_DIM, past), F32),
                        pltpu.VMEM((past, KV_LORA), BF16),
                        pltpu.VMEM((MLA_HEADS, past), F32),
                        pltpu.SemaphoreType.DMA((2, 2))])
    return pl.pallas_call(
        functools.partial(_attn_sample_kernel, n_pages=n_pages, tile=min(tile, past)),
        grid_spec=grid_spec,
        out_shape=jax.ShapeDtypeStruct((nb, MLA_HEADS, KV_LORA), F32),
        compiler_params=_params(1), name="attn_sample")(
            page_table, qlat.reshape(nb, MLA_HEADS, KV_LORA), qh.reshape(nb, MLA_HEADS, HEAD_PAD),
            ckv.reshape(nb, 1, KV_LORA), kpe.reshape(nb, 1, ROPE_DIM), cache_kv, cache_pe_t)


def _uv_kernel(ol_ref, wuv_ref, o_ref):
    lane = lax.broadcasted_iota(jnp.int32, o_ref.shape[:1] + (LANES,), 1)
    for hp in range(MLA_HEADS // 2):
        w = wuv_ref[:, hp * LANES:(hp + 1) * LANES]
        a0 = _dot(ol_ref[:, (2 * hp) * KV_LORA:(2 * hp + 1) * KV_LORA].astype(BF16), w)
        a1 = _dot(ol_ref[:, (2 * hp + 1) * KV_LORA:(2 * hp + 2) * KV_LORA].astype(BF16), w)
        o_ref[:, hp * LANES:(hp + 1) * LANES] = jnp.where(lane < V_DIM, a0, a1).astype(BF16)


def _uv_project(o_lat, wuv):
    n = o_lat.shape[0]
    return pl.pallas_call(
        _uv_kernel, grid=(1,),
        in_specs=[_const_spec(o_lat.shape), _const_spec(wuv.shape)],
        out_specs=pl.BlockSpec((n, MLA_HEADS * V_DIM), lambda i: (0, 0)),
        out_shape=jax.ShapeDtypeStruct((n, MLA_HEADS * V_DIM), BF16),
        compiler_params=_params(1), name="uv_project")(o_lat, wuv)


def _cumsum_rows(v):
    rowi = lax.broadcasted_iota(jnp.int32, v.shape, 0)
    k = 1
    while k < v.shape[0]:
        v = v + jnp.where(rowi >= k, pltpu.roll(v, k, 0), 0.0)
        k *= 2
    return v


def _pair_cols(mat, h0, lane):
    t = mat.shape[0]
    a = jnp.broadcast_to(mat[:, h0:h0 + 1], (t, LANES))
    b = jnp.broadcast_to(mat[:, h0 + 1:h0 + 2], (t, LANES))
    return jnp.where(lane < SSM_HEAD_DIM, a, b)


def _expand_heads(x, e_ref, lane):
    hi = x.astype(BF16).astype(F32)
    r1 = x - hi
    mid = r1.astype(BF16).astype(F32)
    lo = r1 - mid
    packed = jnp.where(lane < SSM_HEADS, hi,
                       jnp.where(lane < 2 * SSM_HEADS, pltpu.roll(mid, SSM_HEADS, 1),
                                 jnp.where(lane < 3 * SSM_HEADS, pltpu.roll(lo, 2 * SSM_HEADS, 1), 0.0)))
    return _dot(packed.astype(BF16), e_ref[...])


def _head_expander(width):
    r = jnp.arange(LANES)[:, None]
    c = jnp.arange(SSM_HEADS * width)[None, :]
    return ((r < 3 * SSM_HEADS) & (r % SSM_HEADS == c // width)).astype(BF16)


def _ssd_prompt_kernel(xs_ref, bc_ref, z_ref, dt_ref, dtb_ref, alog_ref, dskip_ref, gssm_ref, e128_ref, e64_ref,
                       y_ref, state_ref, state):
    t = SSD_CHUNK
    c = pl.program_id(1)

    @pl.when(c == 0)
    def _():
        state[...] = jnp.zeros(state.shape, F32)

    dt = _softplus(dt_ref[...] + dtb_ref[...])
    a = -jnp.exp(alog_ref[...])
    cs = _cumsum_rows(dt * a) * _LOG2E
    cs_last = cs[t - 1:t, :]
    ecs = jnp.exp2(cs)
    wloc = dt * jnp.exp2(cs_last - cs)
    cs_t = cs.T
    dt_t = dt.T
    cdec = jnp.broadcast_to(jnp.exp2(cs_t[:, t - 1:t]), (LANES, LANES))

    lane = lax.broadcasted_iota(jnp.int32, (t, LANES), 1)
    tri = lax.broadcasted_iota(jnp.int32, (t, t), 0) >= lax.broadcasted_iota(jnp.int32, (t, t), 1)
    hpg = SSM_HEADS // SSM_GROUPS
    gw = hpg * SSM_HEAD_DIM
    c_off = SSM_GROUPS * D_STATE
    cs_x = _expand_heads(cs, e128_ref, lane)
    ecs_x = _expand_heads(ecs, e64_ref, lane)
    wloc_x = _expand_heads(wloc, e64_ref, lane)

    for g in range(SSM_GROUPS):
        bm_b = bc_ref[:, g * D_STATE:(g + 1) * D_STATE]
        cm_b = bc_ref[:, c_off + g * D_STATE:c_off + (g + 1) * D_STATE]
        cb = lax.dot_general(cm_b, bm_b, _NT, preferred_element_type=F32)
        prev_b = state[g * gw:(g + 1) * gw, :].astype(BF16)
        yoff = lax.dot_general(cm_b, prev_b, _NT, preferred_element_type=F32)
        ys = []
        for pr in range(hpg // 2):
            h0 = g * hpg + 2 * pr
            x2 = xs_ref[:, h0 * SSM_HEAD_DIM:(h0 + 2) * SSM_HEAD_DIM]
            x2_b = x2.astype(BF16)
            yd = []
            for h in (h0, h0 + 1):
                seg = cs_x[:, h * LANES:(h + 1) * LANES] - cs_t[h:h + 1, :]
                dec = jnp.exp2(jnp.where(tri, seg, -jnp.inf))
                mh = (cb * dec * dt_t[h:h + 1, :]).astype(BF16)
                yd.append(_dot(mh, x2_b))
            pair = slice(h0 * SSM_HEAD_DIM, (h0 + 2) * SSM_HEAD_DIM)
            y2 = jnp.where(lane < SSM_HEAD_DIM, yd[0], yd[1])
            y2 = y2 + yoff[:, 2 * pr * SSM_HEAD_DIM:(2 * pr + 2) * SSM_HEAD_DIM] * ecs_x[:, pair]
            y2 = y2 + dskip_ref[:, pair] * x2
            ys.append(y2)
            xw = (x2 * wloc_x[:, pair]).astype(BF16)
            loc = lax.dot_general(xw, bm_b, _TN, preferred_element_type=F32)
            r0 = h0 * SSM_HEAD_DIM
            for k, h in enumerate((h0, h0 + 1)):
                rs = slice(r0 + k * SSM_HEAD_DIM, r0 + (k + 1) * SSM_HEAD_DIM)
                state[rs, :] = (state[rs, :] * cdec[h:h + 1, :]
                                + loc[k * SSM_HEAD_DIM:(k + 1) * SSM_HEAD_DIM, :])
        yg = jnp.concatenate(ys, axis=1) * _silu(z_ref[:, g * gw:(g + 1) * gw])
        y_ref[:, g * gw:(g + 1) * gw] = (_rms_scale(yg) * gssm_ref[:, g * gw:(g + 1) * gw]).astype(BF16)

    state_ref[0] = state[...]


def _ssd_prompt(xs, bc, z, dt, dtb, alog, dskip, gssm, *, batch, seq):
    nc = seq // SSD_CHUNK
    row = lambda w: pl.BlockSpec((SSD_CHUNK, w), lambda b, c: (b * nc + c, 0))
    e128, e64 = _head_expander(LANES), _head_expander(SSM_HEAD_DIM)
    return pl.pallas_call(
        _ssd_prompt_kernel, grid=(batch, nc),
        in_specs=[row(D_INNER), row(CONV_DIM - D_INNER), row(D_INNER), row(LANES)]
        + [_const_spec(a.shape) for a in (dtb, alog, dskip, gssm, e128, e64)],
        out_specs=[row(D_INNER), pl.BlockSpec((1, D_INNER, D_STATE), lambda b, c: (b, 0, 0))],
        out_shape=[jax.ShapeDtypeStruct((batch * seq, D_INNER), BF16),
                   jax.ShapeDtypeStruct((batch, D_INNER, D_STATE), F32)],
        scratch_shapes=[pltpu.VMEM((D_INNER, D_STATE), F32)],
        compiler_params=_params(2), name="ssd_prompt")(xs, bc, z, dt, dtb, alog, dskip, gssm, e128, e64)


def _ssd_sample_pre_kernel(xbc_ref, cprev_ref, dt_ref, wconv_ref, bconv_ref, dtb_ref, alog_ref,
                           cnew_ref, xs_ref, xdt_t_ref, bm_ref, cm_ref, dec_ref):
    tail = CONV_W - 1
    acc = bconv_ref[...]
    for k in range(tail):
        acc = acc + cprev_ref[:, k * CONV_DIM:(k + 1) * CONV_DIM] * wconv_ref[k:k + 1, :]
    x_new = xbc_ref[...]
    acc = acc + x_new * wconv_ref[tail:tail + 1, :]
    u = _silu(acc)
    for k in range(1, tail):
        cnew_ref[:, (k - 1) * CONV_DIM:k * CONV_DIM] = cprev_ref[:, k * CONV_DIM:(k + 1) * CONV_DIM]
    cnew_ref[:, (tail - 1) * CONV_DIM:] = x_new
    xs = u[:, :D_INNER]
    xs_ref[...] = xs
    bm_ref[...] = u[:, D_INNER:D_INNER + SSM_GROUPS * D_STATE]
    cm_ref[...] = u[:, D_INNER + SSM_GROUPS * D_STATE:]
    dt = _softplus(dt_ref[...] + dtb_ref[...])
    dec_ref[...] = jnp.exp(dt * (-jnp.exp(alog_ref[...])))
    nb = xs.shape[0]
    lane = lax.broadcasted_iota(jnp.int32, (nb, LANES), 1)
    for pr in range(SSM_HEADS // 2):
        x2 = xs[:, pr * LANES:(pr + 1) * LANES] * _pair_cols(dt, 2 * pr, lane)
        xdt_t_ref[pr * LANES:(pr + 1) * LANES, :] = x2.T


def _ssd_sample_pre(xbc, cprev_flat, dt, wconv, bconv, dtb, alog):
    nb = xbc.shape[0]
    ins = (xbc, cprev_flat, dt, wconv, bconv, dtb, alog)
    shapes = [((nb, (CONV_W - 1) * CONV_DIM), F32), ((nb, D_INNER), F32), ((D_INNER, nb), F32),
              ((nb, SSM_GROUPS * D_STATE), F32), ((nb, SSM_GROUPS * D_STATE), F32), ((nb, LANES), F32)]
    return pl.pallas_call(
        _ssd_sample_pre_kernel, grid=(1,),
        in_specs=[_const_spec(a.shape) for a in ins],
        out_specs=[pl.BlockSpec(s, lambda i: (0, 0)) for s, _ in shapes],
        out_shape=[jax.ShapeDtypeStruct(s, d) for s, d in shapes],
        compiler_params=_params(1), name="ssd_sample_pre")(*ins)


def _ssd_sample_state_kernel(dec_ref, st_ref, xdt_t_ref, bm_ref, cm_ref, xs_ref, z_ref, dskip_ref, gssm_ref,
                             st_out_ref, y_ref, y_t):
    b = pl.program_id(0)
    nb = pl.num_programs(0)
    hpg = SSM_HEADS // SSM_GROUPS
    gw = hpg * SSM_HEAD_DIM
    lane = lax.broadcasted_iota(jnp.int32, (gw, nb), 1)
    is_b = lane == b
    groups = [slice(g * gw, (g + 1) * gw) for g in range(SSM_GROUPS)]
    x_cols = [jnp.sum(jnp.where(is_b, xdt_t_ref[rows, :], 0.0), axis=1, keepdims=True) for rows in groups]
    y_cols = []
    for g, rows in enumerate(groups):
        pieces = []
        for hh in range(hpg):
            h = g * hpg + hh
            rs = slice(h * SSM_HEAD_DIM, (h + 1) * SSM_HEAD_DIM)
            pieces.append(st_ref[0, rs, :] * dec_ref[b, h])
        new = jnp.concatenate(pieces, axis=0) + x_cols[g] * bm_ref[0, g:g + 1, :]
        st_out_ref[0, rows, :] = new
        y_cols.append(jnp.sum(new * cm_ref[0, g:g + 1, :], axis=1, keepdims=True))
    for rows, y_col in zip(groups, y_cols):
        y_t[rows, :] = jnp.where(is_b, y_col, y_t[rows, :])

    @pl.when(b == nb - 1)
    def _():
        for g in range(SSM_GROUPS):
            cols = slice(g * gw, (g + 1) * gw)
            y = y_t[cols, :].T + dskip_ref[:, cols] * xs_ref[:, cols]
            y = y * _silu(z_ref[:, cols])
            y_ref[:, cols] = (_rms_scale(y) * gssm_ref[:, cols]).astype(BF16)


def _ssd_sample_state(dec, state, xdt_t, bm, cm, xs, z, dskip, gssm):
    nb = state.shape[0]
    grid_spec = pl.GridSpec(
        grid=(nb,),
        in_specs=[pl.BlockSpec(memory_space=pltpu.SMEM),
                  pl.BlockSpec((1, D_INNER, D_STATE), lambda i: (i, 0, 0)),
                  _const_spec(xdt_t.shape),
                  pl.BlockSpec((1, SSM_GROUPS, D_STATE), lambda i: (i, 0, 0)),
                  pl.BlockSpec((1, SSM_GROUPS, D_STATE), lambda i: (i, 0, 0)),
                  _const_spec(xs.shape), _const_spec(z.shape), _const_spec(dskip.shape), _const_spec(gssm.shape)],
        out_specs=[pl.BlockSpec((1, D_INNER, D_STATE), lambda i: (i, 0, 0)),
                   pl.BlockSpec((nb, D_INNER), lambda i: (0, 0))],
        scratch_shapes=[pltpu.VMEM((D_INNER, nb), F32)])
    return pl.pallas_call(
        _ssd_sample_state_kernel, grid_spec=grid_spec,
        out_shape=[jax.ShapeDtypeStruct(state.shape, F32), jax.ShapeDtypeStruct((nb, D_INNER), BF16)],
        compiler_params=_params(1), name="ssd_sample_state")(
            dec, state, xdt_t, bm.reshape(nb, SSM_GROUPS, D_STATE), cm.reshape(nb, SSM_GROUPS, D_STATE),
            xs, z, dskip, gssm)


def _post_kernel(x_ref, oa_ref, os_ref, sga_ref, sgb_ref, wa_ref, ws_ref, wo_ref, gpm_ref, gpre_ref,
                 wup_ref, wdn_ref, gpost_ref, y_ref, *, ff_chunk):
    merged = sga_ref[...] * _dot(oa_ref[...], wa_ref[...]) + sgb_ref[...] * _dot(os_ref[...], ws_ref[...])
    mix = _dot(merged.astype(BF16), wo_ref[...])
    x1 = x_ref[...] + _rms_scale(mix) * gpm_ref[...]
    h2 = (_rms_scale(x1) * gpre_ref[...]).astype(BF16)
    f = None
    for c in range(D_FF // ff_chunk):
        cs = slice(c * ff_chunk, (c + 1) * ff_chunk)
        u = jnp.square(jnp.maximum(_dot(h2, wup_ref[:, cs]), 0.0)).astype(BF16)
        d = _dot(u, wdn_ref[cs, :])
        f = d if f is None else f + d
    y_ref[...] = x1 + _rms_scale(f) * gpost_ref[...]


def _post(x, oa, os_, sga, sgb, wa, ws, wo, gpm, gpre, wup, wdn, gpost, *, tm, name, ff_chunk=1024):
    n = x.shape[0]
    row = lambda w: pl.BlockSpec((tm, w), lambda i: (i, 0))
    consts = (wa, ws, wo, gpm, gpre, wup, wdn, gpost)
    return pl.pallas_call(
        functools.partial(_post_kernel, ff_chunk=ff_chunk), grid=(n // tm,),
        in_specs=[row(D_MODEL), row(MLA_HEADS * V_DIM), row(D_INNER), row(D_MODEL), row(D_MODEL)]
        + [_const_spec(a.shape) for a in consts],
        out_specs=row(D_MODEL),
        out_shape=jax.ShapeDtypeStruct((n, D_MODEL), F32),
        compiler_params=_params(1), name=name)(x, oa, os_, sga, sgb, *consts)


def _rope_tables(pos):
    half = ROPE_DIM // 2
    inv_freq = ROPE_THETA ** (-jnp.arange(half, dtype=F32) / half)
    ang = pos.astype(F32)[:, None] * inv_freq[None, :]
    cos, sin = jnp.cos(ang), jnp.sin(ang)
    n = pos.shape[0]
    ones_lo = jnp.ones((n, NOPE_DIM), F32)
    pad_hi = jnp.ones((n, HEAD_PAD - NOPE_DIM - ROPE_DIM), F32)
    zeros = lambda w: jnp.zeros((n, w), F32)
    c = jnp.concatenate([ones_lo, cos, cos, pad_hi], axis=1)
    s1 = jnp.concatenate([zeros(NOPE_DIM), -sin, zeros(HEAD_PAD - NOPE_DIM - half)], axis=1)
    s2 = jnp.concatenate([zeros(NOPE_DIM + half), sin, zeros(HEAD_PAD - NOPE_DIM - ROPE_DIM)], axis=1)
    return c, s1, s2


def _pad_cols(w, lo, total):
    return jnp.pad(w, ((0, 0), (lo, total - lo - w.shape[1])))


def _row(v):
    return v.reshape(1, -1)


def kernel(x_prompt, x_sample, cache_kv_latent, cache_k_rope, state_ssm, state_conv, page_table, w_in, g_pre_mix, g_q, w_uq, g_kv, w_uk, w_uv, w_conv, b_conv, dt_bias, a_log, d_skip, g_ssm, w_br_attn, w_br_ssm, w_out, g_post_mix, g_pre_mlp, w_up, w_down, g_post_mlp):
    depth = w_in.shape[0]
    assert depth == 1
    bp, sp, _ = x_prompt.shape
    bs, ss, _ = x_sample.shape
    assert ss == 1
    past = page_table.shape[1] * PAGE_SIZE
    l = 0

    o_q, o_kv, o_pe = 0, Q_LORA, Q_LORA + KV_LORA
    o_z = o_pe + ROPE_DIM
    o_xbc = o_z + D_INNER
    o_dt = o_xbc + CONV_DIM
    o_ga = o_dt + SSM_HEADS
    o_gb = o_ga + D_MODEL
    wi = w_in[l]
    wlat = jnp.concatenate([wi[:, o_q:o_pe], _pad_cols(wi[:, o_pe:o_z], NOPE_DIM, HEAD_PAD)], axis=1).astype(BF16)
    wssm = jnp.concatenate([wi[:, o_z:o_dt], _pad_cols(wi[:, o_dt:o_ga], 0, LANES), wi[:, o_ga:]], axis=1).astype(BF16)
    wuq = jnp.pad(w_uq[l], ((0, 0), (0, 0), (0, HEAD_PAD - NOPE_DIM - ROPE_DIM))).reshape(Q_LORA, -1).astype(BF16)
    wuk = jnp.pad(w_uk[l], ((0, 0), (0, 0), (0, HEAD_PAD - NOPE_DIM))).reshape(KV_LORA, -1).astype(BF16)
    wuv = w_uv[l].reshape(KV_LORA, -1).astype(BF16)
    wv4 = w_uv[l].reshape(KV_LORA, MLA_HEADS // 2, 2, V_DIM)
    zv = jnp.zeros_like(wv4[:, :, 0])
    wuv_aug = jnp.stack([wv4[:, :, 0], zv, zv, wv4[:, :, 1]], axis=2).reshape(KV_LORA, -1).astype(BF16)
    v_ones = jnp.tile(jnp.concatenate([jnp.zeros((V_DIM,), F32), jnp.ones((2 * V_DIM,), F32),
                                       jnp.zeros((V_DIM,), F32)]), MLA_HEADS // 2).reshape(1, -1)
    gpre, gq, gkv = _row(g_pre_mix[l]), _row(g_q[l]), _row(g_kv[l])
    wconv, bconv = w_conv[l], _row(b_conv[l])
    dtb = _pad_cols(_row(dt_bias[l]), 0, LANES)
    alog = _pad_cols(_row(a_log[l]), 0, LANES)
    dskip = _row(jnp.repeat(d_skip[l], SSM_HEAD_DIM))
    gssm = _row(g_ssm[l])
    post_w = (w_br_attn[l].astype(BF16), w_br_ssm[l].astype(BF16), w_out[l].astype(BF16), _row(g_post_mix[l]),
              _row(g_pre_mlp[l]), w_up[l].astype(BF16), w_down[l].astype(BF16), _row(g_post_mlp[l]))

    xp = x_prompt.reshape(bp * sp, D_MODEL)
    tm_p = min(512, sp)
    tabs_p = _rope_tables(jnp.arange(sp, dtype=jnp.int32))
    qh_p, ckv_p, kpe_p, kh_p, v_p = _mla_proj(xp, gpre, wlat, gq, wuq, gkv, tabs_p, wuk, (wuv_aug, v_ones),
                                              prompt=True, tm=tm_p, tab_blocks=sp // tm_p)
    z_p, xsc_p, bc_p, dt_p, sga_p, sgb_p, conv_p = _ssm_proj(xp, gpre, wssm, wconv, bconv, tm=min(256, sp), seq=sp,
                                                             name="ssm_proj_prompt")
    oa_p = _attn_prompt(qh_p.reshape(bp, sp, -1), kh_p.reshape(bp, sp, -1), v_p.reshape(bp, sp, -1),
                        tq=min(256, sp))
    os_p, ssm_p = _ssd_prompt(xsc_p, bc_p, z_p, dt_p, dtb, alog, dskip, gssm, batch=bp, seq=sp)
    y_p = _post(xp, oa_p.reshape(bp * sp, -1), os_p, sga_p, sgb_p, *post_w, tm=min(256, sp), name="post_prompt")

    xs_in = x_sample.reshape(bs, D_MODEL)
    tabs_s = _rope_tables(jnp.full((bs,), past, dtype=jnp.int32))
    qh_s, ckv_s, kpe_s, qlat_s = _mla_proj(xs_in, gpre, wlat, gq, wuq, gkv, tabs_s, wuk, None,
                                           prompt=False, tm=bs, tab_blocks=1)
    z_s, xbc_s, dt_s, sga_s, sgb_s = _ssm_proj(xs_in, gpre, wssm, tm=bs, name="ssm_proj_sample")
    o_lat = _attn_sample(page_table, qlat_s, qh_s, ckv_s, kpe_s, cache_kv_latent[l],
                         jnp.swapaxes(cache_k_rope[l], 1, 2))
    oa_s = _uv_project(o_lat.reshape(bs, MLA_HEADS * KV_LORA), wuv)
    cnew_s, xs_s, xdt_t, bm_s, cm_s, dec_s = _ssd_sample_pre(
        xbc_s, state_conv[l].reshape(bs, (CONV_W - 1) * CONV_DIM), dt_s, wconv, bconv, dtb, alog)
    ssm_s, os_s = _ssd_sample_state(dec_s[:, :SSM_HEADS], state_ssm[l].reshape(bs, D_INNER, D_STATE),
                                    xdt_t, bm_s, cm_s, xs_s, z_s, dskip, gssm)
    y_s = _post(xs_in, oa_s, os_s, sga_s, sgb_s, *post_w, tm=bs, name="post_sample")

    hshape = (SSM_HEADS, SSM_HEAD_DIM, D_STATE)
    return (y_p.reshape(bp, sp, D_MODEL), y_s.reshape(bs, 1, D_MODEL),
            ckv_p.reshape(1, bp, sp, KV_LORA), kpe_p.reshape(1, bp, sp, ROPE_DIM),
            ssm_p.reshape(1, bp, *hshape), conv_p.reshape(1, bp, CONV_W - 1, CONV_DIM),
            ckv_s.reshape(1, bs, 1, KV_LORA), kpe_s.reshape(1, bs, 1, ROPE_DIM),
            ssm_s.reshape(1, bs, *hshape), cnew_s.reshape(1, bs, CONV_W - 1, CONV_DIM))
```

```python
import functools
import math

import jax
import jax.numpy as jnp
from jax import lax
from jax.experimental import pallas as pl
from jax.experimental.pallas import tpu as pltpu

F32 = jnp.float32
BF16 = jnp.bfloat16

D_MODEL = 1024
MLA_HEADS = 16
Q_LORA = 384
KV_LORA = 256
NOPE_DIM = 64
ROPE_DIM = 32
V_DIM = 64
ROPE_THETA = 10000.0
SM_SCALE = 1.0 / math.sqrt(NOPE_DIM + ROPE_DIM)
_LOG2E = math.log2(math.e)
_EXP2_SCALE = SM_SCALE * _LOG2E
PAGE_SIZE = 128
D_INNER = 2048
SSM_HEAD_DIM = 64
SSM_HEADS = D_INNER // SSM_HEAD_DIM
SSM_GROUPS = 8
D_STATE = 128
CONV_W = 4
CONV_DIM = D_INNER + 2 * SSM_GROUPS * D_STATE
SSD_CHUNK = 128
D_FF = 4096
EPS = 1e-6

LANES = 128
SUBLANES = 8
HEAD_PAD = LANES
VMEM_LIMIT = 56 * 1024 * 1024

_NT = (((1,), (1,)), ((), ()))
_TN = (((0,), (0,)), ((), ()))


def _rms_scale(x):
    return x * lax.rsqrt(jnp.mean(x * x, axis=-1, keepdims=True) + EPS)


def _sigmoid(x):
    return 1.0 / (1.0 + jnp.exp(-x))


def _silu(x):
    return x * _sigmoid(x)


def _softplus(x):
    return jnp.maximum(x, 0.0) + jnp.log(1.0 + jnp.exp(-jnp.abs(x)))


def _dot(a, b):
    return jnp.dot(a, b, preferred_element_type=F32)


def _const_spec(shape):
    nd = len(shape)
    return pl.BlockSpec(shape, lambda *_: (0,) * nd, pipeline_mode=pl.Buffered(1))


def _params(n_grid, vmem=VMEM_LIMIT, flags=None):
    return pltpu.CompilerParams(dimension_semantics=("arbitrary",) * n_grid, vmem_limit_bytes=vmem, flags=flags)


def _rope_tile(t, c, s1, s2):
    return t * c + pltpu.roll(t, LANES - ROPE_DIM // 2, 1) * s1 + pltpu.roll(t, ROPE_DIM // 2, 1) * s2


def _mla_proj_kernel(*refs, prompt):
    (x_ref, gpre_ref, wlat_ref, gq_ref, wuq_ref, gkv_ref, rc_ref, rs1_ref, rs2_ref, wuk_ref) = refs[:10]
    if prompt:
        wuv_ref, vone_ref, qh_ref, ckv_ref, kpe_ref, kh_ref, v_ref = refs[10:]
    else:
        qh_ref, ckv_ref, kpe_ref, qlat_ref = refs[10:]
    h = (_rms_scale(x_ref[...]) * gpre_ref[...]).astype(BF16)
    lat = _dot(h, wlat_ref[...])
    cq = (_rms_scale(lat[:, :Q_LORA]) * gq_ref[...]).astype(BF16)
    ckv = _rms_scale(lat[:, Q_LORA:Q_LORA + KV_LORA]) * gkv_ref[...]
    ckv_ref[...] = ckv
    c, s1, s2 = rc_ref[...], rs1_ref[...], rs2_ref[...]
    kpe = _rope_tile(lat[:, Q_LORA + KV_LORA:], c, s1, s2)
    kpe_ref[...] = kpe[:, NOPE_DIM:NOPE_DIM + ROPE_DIM]
    q = _dot(cq, wuq_ref[...])
    ckv_b = ckv.astype(BF16)
    if prompt:
        k = _dot(ckv_b, wuk_ref[...])
        v_ref[...] = (_dot(ckv_b, wuv_ref[...]) + vone_ref[...]).astype(BF16)
    for hd in range(MLA_HEADS):
        sl = slice(hd * HEAD_PAD, (hd + 1) * HEAD_PAD)
        qh = _rope_tile(q[:, sl], c, s1, s2)
        if prompt:
            qh = qh * _EXP2_SCALE
        qh = qh.astype(BF16)
        qh_ref[:, sl] = qh
        if prompt:
            kh_ref[:, sl] = (k[:, sl] + kpe).astype(BF16)
        else:
            qlat_ref[:, hd * KV_LORA:(hd + 1) * KV_LORA] = lax.dot_general(
                qh, wuk_ref[:, sl], _NT, preferred_element_type=F32)


def _mla_proj(x, gpre, wlat, gq, wuq, gkv, rope_tabs, wuk, wuv, *, prompt, tm, tab_blocks):
    n = x.shape[0]
    hp = MLA_HEADS * HEAD_PAD
    row = lambda w: pl.BlockSpec((tm, w), lambda i: (i, 0))
    tab = pl.BlockSpec((tm, LANES), lambda i: (i % tab_blocks, 0))
    in_specs = [row(D_MODEL), _const_spec(gpre.shape), _const_spec(wlat.shape), _const_spec(gq.shape),
                _const_spec(wuq.shape), _const_spec(gkv.shape), tab, tab, tab, _const_spec(wuk.shape)]
    args = [x, gpre, wlat, gq, wuq, gkv, *rope_tabs, wuk]
    out_shape = [jax.ShapeDtypeStruct((n, hp), BF16), jax.ShapeDtypeStruct((n, KV_LORA), F32),
                 jax.ShapeDtypeStruct((n, ROPE_DIM), F32)]
    out_specs = [row(hp), row(KV_LORA), row(ROPE_DIM)]
    if prompt:
        wuv_aug, v_ones = wuv
        in_specs += [_const_spec(wuv_aug.shape), _const_spec(v_ones.shape)]
        args += [wuv_aug, v_ones]
        out_shape += [jax.ShapeDtypeStruct((n, hp), BF16), jax.ShapeDtypeStruct((n, hp), BF16)]
        out_specs += [row(hp), row(hp)]
    else:
        out_shape.append(jax.ShapeDtypeStruct((n, MLA_HEADS * KV_LORA), F32))
        out_specs.append(row(MLA_HEADS * KV_LORA))
    return pl.pallas_call(
        functools.partial(_mla_proj_kernel, prompt=prompt),
        grid=(n // tm,), in_specs=in_specs, out_specs=out_specs, out_shape=out_shape,
        compiler_params=_params(1), name="mla_proj_prompt" if prompt else "mla_proj_sample")(*args)


_SSM_OFF_Z = 0
_SSM_OFF_XBC = D_INNER
_SSM_OFF_DT = D_INNER + CONV_DIM
_SSM_OFF_GA = _SSM_OFF_DT + LANES
_SSM_OFF_GB = _SSM_OFF_GA + D_MODEL
_SSM_COLS = _SSM_OFF_GB + D_MODEL


_CONV_TAIL = CONV_W - 1
_CONV_COLS = 512


def _ssm_proj_kernel(*refs, conv, blocks_per_seq):
    if conv:
        (x_ref, gpre_ref, w_ref, wconv_ref, bconv_ref,
         z_ref, xs_ref, bc_ref, dt_ref, sga_ref, sgb_ref, tail_ref, h_scr) = refs[:13]
        xpads = refs[13:]
    else:
        x_ref, gpre_ref, w_ref, z_ref, xbc_ref, dt_ref, sga_ref, sgb_ref, h_scr = refs
    tm = x_ref.shape[0]
    h_scr[...] = (_rms_scale(x_ref[...]) * gpre_ref[...]).astype(BF16)

    def project(off, width):
        return _dot(h_scr[...], w_ref[:, off:off + width])

    def plain(out_ref, off, width, act):
        def run(lo):
            w = min(_CONV_COLS, width - lo)
            out_ref[:, lo:lo + w] = act(project(off + lo, w))
        return [functools.partial(run, lo) for lo in range(0, width, _CONV_COLS)]

    ident = lambda v: v
    others = (plain(z_ref, _SSM_OFF_Z, D_INNER, ident) + plain(sga_ref, _SSM_OFF_GA, D_MODEL, _sigmoid)
              + plain(sgb_ref, _SSM_OFF_GB, D_MODEL, _sigmoid) + plain(dt_ref, _SSM_OFF_DT, LANES, ident))
    if not conv:
        xbc_ref[...] = project(_SSM_OFF_XBC, CONV_DIM)
        for f in others:
            f()
        return

    first = pl.program_id(0) % blocks_per_seq == 0
    n_chunks = CONV_DIM // _CONV_COLS
    n_tiles = tm // SUBLANES

    @pl.when(first)
    def _():
        for xp in xpads:
            xp[0:SUBLANES, :] = jnp.zeros((SUBLANES, _CONV_COLS), F32)

    @pl.when(jnp.logical_not(first))
    def _():
        for xp in xpads:
            xp[0:SUBLANES, :] = xp[tm:tm + SUBLANES, :]

    sub = lax.broadcasted_iota(jnp.int32, (n_tiles, SUBLANES, _CONV_COLS), 1)

    def project_chunk(i):
        xpads[i][SUBLANES:, :] = project(_SSM_OFF_XBC + i * _CONV_COLS, _CONV_COLS)

    project_chunk(0)
    for i in range(n_chunks):
        lo = i * _CONV_COLS
        cols = slice(lo, lo + _CONV_COLS)
        if i + 1 < n_chunks:
            project_chunk(i + 1)
        if i < len(others):
            others[i]()
        tiles = xpads[i][...].reshape(n_tiles + 1, SUBLANES, _CONV_COLS)
        acc = bconv_ref[:, cols]
        for k in range(CONV_W):
            s = _CONV_TAIL - k
            if s == 0:
                shifted = tiles[1:]
            else:
                r = pltpu.roll(tiles, s, 1)
                shifted = jnp.where(sub >= s, r[1:], r[:-1])
            acc = acc + shifted.reshape(tm, _CONV_COLS) * wconv_ref[k:k + 1, cols]
        u = _silu(acc)
        if lo < D_INNER:
            xs_ref[:, cols] = u
        else:
            bc_ref[:, lo - D_INNER:lo - D_INNER + _CONV_COLS] = u.astype(BF16)
        tail_ref[0, :, cols] = xpads[i][SUBLANES + tm - _CONV_TAIL:SUBLANES + tm, :]
    for f in others[n_chunks:]:
        f()


def _ssm_proj(x, gpre, wssm, wconv=None, bconv=None, *, tm, seq=None, name):
    n = x.shape[0]
    conv = wconv is not None
    row = lambda w, dt=F32: (pl.BlockSpec((tm, w), lambda i: (i, 0)), jax.ShapeDtypeStruct((n, w), dt))
    in_specs = [pl.BlockSpec((tm, D_MODEL), lambda i: (i, 0)), _const_spec(gpre.shape), _const_spec(wssm.shape)]
    args = [x, gpre, wssm]
    scratch = [pltpu.VMEM((tm, D_MODEL), BF16)]
    if conv:
        bps = seq // tm
        in_specs += [_const_spec(wconv.shape), _const_spec(bconv.shape)]
        args += [wconv, bconv]
        outs = [row(D_INNER), row(D_INNER), row(CONV_DIM - D_INNER, BF16), row(LANES), row(D_MODEL), row(D_MODEL),
                (pl.BlockSpec((1, _CONV_TAIL, CONV_DIM), lambda i: (i // bps, 0, 0)),
                 jax.ShapeDtypeStruct((n // seq, _CONV_TAIL, CONV_DIM), F32))]
        scratch += [pltpu.VMEM((SUBLANES + tm, _CONV_COLS), F32)] * (CONV_DIM // _CONV_COLS)
    else:
        bps = 1
        outs = [row(D_INNER), row(CONV_DIM), row(LANES), row(D_MODEL), row(D_MODEL)]
    return pl.pallas_call(
        functools.partial(_ssm_proj_kernel, conv=conv, blocks_per_seq=bps), grid=(n // tm,),
        in_specs=in_specs, out_specs=[o[0] for o in outs], out_shape=[o[1] for o in outs],
        scratch_shapes=scratch, compiler_params=_params(1), name=name)(*args)


def _fold_lanes(x, op):
    out = x[:, :LANES]
    for c in range(1, x.shape[1] // LANES):
        out = op(out, x[:, c * LANES:(c + 1) * LANES])
    return out


def _attn_prompt_kernel(q_ref, k_ref, v_ref, o_ref, s_scr, p_scr, o_scr, *, seq, tq):
    lane = lax.broadcasted_iota(jnp.int32, (tq, LANES), 1)
    row = lax.broadcasted_iota(jnp.int32, (tq, tq), 0)
    col = lax.broadcasted_iota(jnp.int32, (tq, tq), 1)
    causal = col <= row
    chains = [(qi, j) for qi in range(seq // tq) for j in range(2)]
    row_max = {}

    def scores(c):
        qi, j = chains[c]
        hs = slice(j * HEAD_PAD, (j + 1) * HEAD_PAD)
        q = q_ref[0, qi * tq:(qi + 1) * tq, hs]
        mx = None
        for kt in range(qi + 1):
            ks = slice(kt * tq, (kt + 1) * tq)
            s = lax.dot_general(q, k_ref[0, ks, hs], _NT, preferred_element_type=F32)
            if kt == qi:
                s = jnp.where(causal, s, -jnp.inf)
            s_scr[c % 2, :, ks] = s
            m2 = _fold_lanes(s, jnp.maximum)
            mx = m2 if mx is None else jnp.maximum(mx, m2)
        row_max[c] = jnp.max(mx, axis=-1, keepdims=True)

    def numerators(c):
        qi, _ = chains[c]
        m = row_max.pop(c)
        for kt in range(qi + 1):
            ks = slice(kt * tq, (kt + 1) * tq)
            p_scr[c % 2, :, ks] = jnp.exp2(s_scr[c % 2, :, ks] - m).astype(BF16)

    def weighted_values(c):
        qi, j = chains[c]
        lo, hi = qi * tq, (qi + 1) * tq
        acc = _dot(p_scr[c % 2, :, :hi], v_ref[0, :hi, j * LANES:(j + 1) * LANES])
        out = acc / pltpu.roll(acc, V_DIM, 1)
        if j == 0:
            o_scr[...] = out
        else:
            o_ref[0, lo:hi, :] = jnp.where(lane < V_DIM, o_scr[...], out).astype(BF16)

    n = len(chains)
    for step in range(n + 2):
        if step < n:
            scores(step)
        if 1 <= step <= n:
            numerators(step - 1)
        if step >= 2:
            weighted_values(step - 2)


def _attn_prompt(qh, kh, v, *, tq=256):
    b, s, _ = qh.shape
    pair = 2 * HEAD_PAD
    return pl.pallas_call(
        functools.partial(_attn_prompt_kernel, seq=s, tq=tq),
        grid=(b, MLA_HEADS // 2),
        in_specs=[pl.BlockSpec((1, s, pair), lambda i, j: (i, 0, j)),
                  pl.BlockSpec((1, s, pair), lambda i, j: (i, 0, j)),
                  pl.BlockSpec((1, s, pair), lambda i, j: (i, 0, j))],
        out_specs=pl.BlockSpec((1, s, 2 * V_DIM), lambda i, j: (i, 0, j)),
        out_shape=jax.ShapeDtypeStruct((b, s, MLA_HEADS * V_DIM), BF16),
        scratch_shapes=[pltpu.VMEM((2, tq, s), F32), pltpu.VMEM((2, tq, s), BF16),
                        pltpu.VMEM((tq, LANES), F32)],
        compiler_params=_params(2), name="attn_prompt")(qh, kh, v)


def _attn_sample_kernel(pt_ref, qlat_ref, qh_ref, ckv_ref, kpe_ref, ckv_prev_ref, ckv_hbm, cpet_hbm, o_ref,
                        kv0, kv1, pe0, pe1, kvb0, kvb1, s0, s1, st0, st1, sem, *, n_pages, tile, nb):
    g = pl.program_id(0)
    past = n_pages * PAGE_SIZE
    n_tiles = past // tile
    kv, pe, kvb, sc_scr, st = (kv0, kv1), (pe0, pe1), (kvb0, kvb1), (s0, s1), (st0, st1)

    def start_fetch(seq, r):
        for j in range(n_pages):
            pg = pt_ref[seq, j]
            rows = pl.ds(j * PAGE_SIZE, PAGE_SIZE)
            pltpu.make_async_copy(ckv_hbm.at[pg], kv[r].at[rows], sem.at[0, r]).start()
            pltpu.make_async_copy(cpet_hbm.at[pg], pe[r].at[:, rows], sem.at[1, r]).start()

    def wait_fetch(r):
        pltpu.make_async_copy(kv[r], kv[r], sem.at[0, r]).wait()
        pltpu.make_async_copy(pe[r], pe[r], sem.at[1, r]).wait()

    @pl.when(g == 0)
    def _():
        start_fetch(0, 0)
        kvb1[...] = jnp.zeros(kvb1.shape, BF16)
        s1[...] = jnp.zeros(s1.shape, F32)
        st1[...] = jnp.zeros(st1.shape, F32)

    def half_step(r):
        h = 2 * g + r
        p_ = 1 - r
        start_fetch(jnp.minimum(h + 1, nb - 1), p_)
        wait_fetch(r)

        qlat = qlat_ref[r]
        qlat_b = qlat.astype(BF16)
        qpe_b = qh_ref[r][:, NOPE_DIM:NOPE_DIM + ROPE_DIM]
        m_prev = st[p_][0][:, :1]
        snew_prev = st[p_][1][:, :1]
        ckv_prev = ckv_prev_ref[r]
        p_new = jnp.exp2((snew_prev - m_prev) * _EXP2_SCALE)
        psum = None
        accs = [p_new * ckv_prev, None]
        mx = None
        for t in range(n_tiles):
            ts = slice(t * tile, (t + 1) * tile)
            kc = kv[r][ts, :].astype(BF16)
            kvb[r][ts, :] = kc
            sc = (lax.dot_general(qlat_b, kc, _NT, preferred_element_type=F32)
                  + _dot(qpe_b, pe[r][:, ts].astype(BF16)))
            sc_scr[r][:, ts] = sc
            mx = sc if mx is None else jnp.maximum(mx, sc)
            p = jnp.exp2((sc_scr[p_][:, ts] - m_prev) * _EXP2_SCALE)
            psum = p if psum is None else psum + p
            d = _dot(p.astype(BF16), kvb[p_][ts, :])
            accs[t % 2] = d if accs[t % 2] is None else accs[t % 2] + d
        ckv_new = ckv_ref[r]
        kpe_new = kpe_ref[r]
        s_new = (jnp.sum(qlat * ckv_new, axis=-1, keepdims=True)
                 + jnp.sum(qpe_b.astype(F32) * kpe_new, axis=-1, keepdims=True))
        m = jnp.maximum(jnp.max(mx, axis=-1, keepdims=True), s_new)
        st[r][0] = jnp.broadcast_to(m, (MLA_HEADS, LANES))
        st[r][1] = jnp.broadcast_to(s_new, (MLA_HEADS, LANES))
        den = p_new + jnp.sum(psum, axis=-1, keepdims=True)
        acc = accs[0] if accs[1] is None else accs[0] + accs[1]
        o_ref[r] = acc / den

    half_step(0)
    half_step(1)

    @pl.when(g == pl.num_programs(0) - 1)
    def _():
        wait_fetch(0)


def _attn_sample(page_table, qlat, qh, ckv, kpe, cache_kv, cache_pe_t, *, tile=256):
    nb, n_pages = page_table.shape
    assert nb % 2 == 0
    past = n_pages * PAGE_SIZE
    last = nb // 2 - 1
    cur = lambda i, pt: (jnp.minimum(i, last), 0, 0)
    pair = lambda w: pl.BlockSpec((2, MLA_HEADS, w), cur)
    vec = lambda w: pl.BlockSpec((2, 1, w), cur)
    slot = lambda shape, dt: [pltpu.VMEM(shape, dt)] * 2
    grid_spec = pltpu.PrefetchScalarGridSpec(
        num_scalar_prefetch=1, grid=(nb // 2 + 1,),
        in_specs=[pair(KV_LORA), pair(HEAD_PAD), vec(KV_LORA), vec(ROPE_DIM),
                  pl.BlockSpec((2, 1, KV_LORA), lambda i, pt: (i, 0, 0)),
                  pl.BlockSpec(memory_space=pl.ANY),
                  pl.BlockSpec(memory_space=pl.ANY)],
        out_specs=pl.BlockSpec((2, MLA_HEADS, KV_LORA), lambda i, pt: (i, 0, 0)),
        scratch_shapes=(slot((past, KV_LORA), F32) + slot((ROPE_DIM, past), F32) + slot((past, KV_LORA), BF16)
                        + slot((MLA_HEADS, past), F32) + slot((2, MLA_HEADS, LANES), F32)
                        + [pltpu.SemaphoreType.DMA((2, 2))]))
    ckv3 = ckv.reshape(nb, 1, KV_LORA)
    out = pl.pallas_call(
        functools.partial(_attn_sample_kernel, n_pages=n_pages, tile=min(tile, past), nb=nb),
        grid_spec=grid_spec,
        out_shape=jax.ShapeDtypeStruct((nb + 2, MLA_HEADS, KV_LORA), F32),
        compiler_params=_params(1), name="attn_sample")(
            page_table, qlat.reshape(nb, MLA_HEADS, KV_LORA), qh.reshape(nb, MLA_HEADS, HEAD_PAD),
            ckv3, kpe.reshape(nb, 1, ROPE_DIM), jnp.pad(ckv3, ((1, 1), (0, 0), (0, 0))), cache_kv, cache_pe_t)
    return out[1:nb + 1]


def _uv_kernel(ol_ref, wuv_ref, o_ref):
    lane = lax.broadcasted_iota(jnp.int32, o_ref.shape[:1] + (LANES,), 1)
    for hp in range(MLA_HEADS // 2):
        w = wuv_ref[:, hp * LANES:(hp + 1) * LANES]
        a0 = _dot(ol_ref[:, (2 * hp) * KV_LORA:(2 * hp + 1) * KV_LORA].astype(BF16), w)
        a1 = _dot(ol_ref[:, (2 * hp + 1) * KV_LORA:(2 * hp + 2) * KV_LORA].astype(BF16), w)
        o_ref[:, hp * LANES:(hp + 1) * LANES] = jnp.where(lane < V_DIM, a0, a1).astype(BF16)


def _uv_project(o_lat, wuv):
    n = o_lat.shape[0]
    return pl.pallas_call(
        _uv_kernel, grid=(1,),
        in_specs=[_const_spec(o_lat.shape), _const_spec(wuv.shape)],
        out_specs=pl.BlockSpec((n, MLA_HEADS * V_DIM), lambda i: (0, 0)),
        out_shape=jax.ShapeDtypeStruct((n, MLA_HEADS * V_DIM), BF16),
        compiler_params=_params(1), name="uv_project")(o_lat, wuv)


def _cumsum_rows(v):
    rowi = lax.broadcasted_iota(jnp.int32, v.shape, 0)
    k = 1
    while k < v.shape[0]:
        v = v + jnp.where(rowi >= k, pltpu.roll(v, k, 0), 0.0)
        k *= 2
    return v


def _pair_cols(mat, h0, lane):
    t = mat.shape[0]
    a = jnp.broadcast_to(mat[:, h0:h0 + 1], (t, LANES))
    b = jnp.broadcast_to(mat[:, h0 + 1:h0 + 2], (t, LANES))
    return jnp.where(lane < SSM_HEAD_DIM, a, b)


def _expand_heads(x, e_ref, lane):
    hi = x.astype(BF16).astype(F32)
    r1 = x - hi
    mid = r1.astype(BF16).astype(F32)
    lo = r1 - mid
    packed = jnp.where(lane < SSM_HEADS, hi,
                       jnp.where(lane < 2 * SSM_HEADS, pltpu.roll(mid, SSM_HEADS, 1),
                                 jnp.where(lane < 3 * SSM_HEADS, pltpu.roll(lo, 2 * SSM_HEADS, 1), 0.0)))
    return _dot(packed.astype(BF16), e_ref[...])


def _head_expander(width):
    r = jnp.arange(LANES)[:, None]
    c = jnp.arange(SSM_HEADS * width)[None, :]
    return ((r < 3 * SSM_HEADS) & (r % SSM_HEADS == c // width)).astype(BF16)


def _ssd_prompt_kernel(xs_ref, bc_ref, z_ref, dt_ref, dtb_ref, alog_ref, dskip_ref, gssm_ref, e128_ref, e64_ref,
                       y_ref, state_ref, state):
    t = SSD_CHUNK
    c = pl.program_id(1)

    @pl.when(c == 0)
    def _():
        state[...] = jnp.zeros(state.shape, F32)

    dt = _softplus(dt_ref[...] + dtb_ref[...])
    a = -jnp.exp(alog_ref[...])
    cs = _cumsum_rows(dt * a) * _LOG2E
    cs_last = cs[t - 1:t, :]
    ecs = jnp.exp2(cs)
    wloc = dt * jnp.exp2(cs_last - cs)
    cs_t = cs.T
    dt_t = dt.T
    cdec = jnp.broadcast_to(jnp.exp2(cs_t[:, t - 1:t]), (LANES, LANES))

    lane = lax.broadcasted_iota(jnp.int32, (t, LANES), 1)
    tri = lax.broadcasted_iota(jnp.int32, (t, t), 0) >= lax.broadcasted_iota(jnp.int32, (t, t), 1)
    hpg = SSM_HEADS // SSM_GROUPS
    gw = hpg * SSM_HEAD_DIM
    c_off = SSM_GROUPS * D_STATE
    cs_x = _expand_heads(cs, e128_ref, lane)
    ecs_x = _expand_heads(ecs, e64_ref, lane)
    wloc_x = _expand_heads(wloc, e64_ref, lane)

    for g in range(SSM_GROUPS):
        bm_b = bc_ref[:, g * D_STATE:(g + 1) * D_STATE]
        cm_b = bc_ref[:, c_off + g * D_STATE:c_off + (g + 1) * D_STATE]
        cb = lax.dot_general(cm_b, bm_b, _NT, preferred_element_type=F32)
        prev_b = state[g * gw:(g + 1) * gw, :].astype(BF16)
        yoff = lax.dot_general(cm_b, prev_b, _NT, preferred_element_type=F32)
        ys = []
        for pr in range(hpg // 2):
            h0 = g * hpg + 2 * pr
            x2 = xs_ref[:, h0 * SSM_HEAD_DIM:(h0 + 2) * SSM_HEAD_DIM]
            x2_b = x2.astype(BF16)
            yd = []
            for h in (h0, h0 + 1):
                seg = cs_x[:, h * LANES:(h + 1) * LANES] - cs_t[h:h + 1, :]
                dec = jnp.exp2(jnp.where(tri, seg, -jnp.inf))
                mh = (cb * dec * dt_t[h:h + 1, :]).astype(BF16)
                yd.append(_dot(mh, x2_b))
            pair = slice(h0 * SSM_HEAD_DIM, (h0 + 2) * SSM_HEAD_DIM)
            y2 = jnp.where(lane < SSM_HEAD_DIM, yd[0], yd[1])
            y2 = y2 + yoff[:, 2 * pr * SSM_HEAD_DIM:(2 * pr + 2) * SSM_HEAD_DIM] * ecs_x[:, pair]
            y2 = y2 + dskip_ref[:, pair] * x2
            ys.append(y2)
            xw = (x2 * wloc_x[:, pair]).astype(BF16)
            loc = lax.dot_general(xw, bm_b, _TN, preferred_element_type=F32)
            r0 = h0 * SSM_HEAD_DIM
            for k, h in enumerate((h0, h0 + 1)):
                rs = slice(r0 + k * SSM_HEAD_DIM, r0 + (k + 1) * SSM_HEAD_DIM)
                state[rs, :] = (state[rs, :] * cdec[h:h + 1, :]
                                + loc[k * SSM_HEAD_DIM:(k + 1) * SSM_HEAD_DIM, :])
        yg = jnp.concatenate(ys, axis=1) * _silu(z_ref[:, g * gw:(g + 1) * gw])
        y_ref[:, g * gw:(g + 1) * gw] = (_rms_scale(yg) * gssm_ref[:, g * gw:(g + 1) * gw]).astype(BF16)

    state_ref[0] = state[...]


def _ssd_prompt(xs, bc, z, dt, dtb, alog, dskip, gssm, *, batch, seq):
    nc = seq // SSD_CHUNK
    row = lambda w: pl.BlockSpec((SSD_CHUNK, w), lambda b, c: (b * nc + c, 0))
    e128, e64 = _head_expander(LANES), _head_expander(SSM_HEAD_DIM)
    return pl.pallas_call(
        _ssd_prompt_kernel, grid=(batch, nc),
        in_specs=[row(D_INNER), row(CONV_DIM - D_INNER), row(D_INNER), row(LANES)]
        + [_const_spec(a.shape) for a in (dtb, alog, dskip, gssm, e128, e64)],
        out_specs=[row(D_INNER), pl.BlockSpec((1, D_INNER, D_STATE), lambda b, c: (b, 0, 0))],
        out_shape=[jax.ShapeDtypeStruct((batch * seq, D_INNER), BF16),
                   jax.ShapeDtypeStruct((batch, D_INNER, D_STATE), F32)],
        scratch_shapes=[pltpu.VMEM((D_INNER, D_STATE), F32)],
        compiler_params=_params(2), name="ssd_prompt")(xs, bc, z, dt, dtb, alog, dskip, gssm, e128, e64)


def _ssd_sample_pre_kernel(xbc_ref, cprev_ref, dt_ref, wconv_ref, bconv_ref, dtb_ref, alog_ref,
                           cnew_ref, xs_ref, xdt_t_ref, bm_ref, cm_ref, dec_ref):
    tail = CONV_W - 1
    acc = bconv_ref[...]
    for k in range(tail):
        acc = acc + cprev_ref[:, k * CONV_DIM:(k + 1) * CONV_DIM] * wconv_ref[k:k + 1, :]
    x_new = xbc_ref[...]
    acc = acc + x_new * wconv_ref[tail:tail + 1, :]
    u = _silu(acc)
    for k in range(1, tail):
        cnew_ref[:, (k - 1) * CONV_DIM:k * CONV_DIM] = cprev_ref[:, k * CONV_DIM:(k + 1) * CONV_DIM]
    cnew_ref[:, (tail - 1) * CONV_DIM:] = x_new
    xs = u[:, :D_INNER]
    xs_ref[...] = xs
    bm_ref[...] = u[:, D_INNER:D_INNER + SSM_GROUPS * D_STATE]
    cm_ref[...] = u[:, D_INNER + SSM_GROUPS * D_STATE:]
    dt = _softplus(dt_ref[...] + dtb_ref[...])
    dec_ref[...] = jnp.exp(dt * (-jnp.exp(alog_ref[...])))
    nb = xs.shape[0]
    lane = lax.broadcasted_iota(jnp.int32, (nb, LANES), 1)
    for pr in range(SSM_HEADS // 2):
        x2 = xs[:, pr * LANES:(pr + 1) * LANES] * _pair_cols(dt, 2 * pr, lane)
        xdt_t_ref[pr * LANES:(pr + 1) * LANES, :] = x2.T


def _ssd_sample_pre(xbc, cprev_flat, dt, wconv, bconv, dtb, alog):
    nb = xbc.shape[0]
    ins = (xbc, cprev_flat, dt, wconv, bconv, dtb, alog)
    shapes = [((nb, (CONV_W - 1) * CONV_DIM), F32), ((nb, D_INNER), F32), ((D_INNER, nb), F32),
              ((nb, SSM_GROUPS * D_STATE), F32), ((nb, SSM_GROUPS * D_STATE), F32), ((nb, LANES), F32)]
    return pl.pallas_call(
        _ssd_sample_pre_kernel, grid=(1,),
        in_specs=[_const_spec(a.shape) for a in ins],
        out_specs=[pl.BlockSpec(s, lambda i: (0, 0)) for s, _ in shapes],
        out_shape=[jax.ShapeDtypeStruct(s, d) for s, d in shapes],
        compiler_params=_params(1), name="ssd_sample_pre")(*ins)


def _ssd_sample_state_kernel(dec_ref, st_ref, xdt_t_ref, bm_ref, cm_ref, xs_ref, z_ref, dskip_ref, gssm_ref,
                             st_out_ref, y_ref, y_t):
    per_step = st_ref.shape[0]
    nb = xs_ref.shape[0]
    hpg = SSM_HEADS // SSM_GROUPS
    gw = hpg * SSM_HEAD_DIM
    lane = lax.broadcasted_iota(jnp.int32, (gw, nb), 1)
    groups = [slice(g * gw, (g + 1) * gw) for g in range(SSM_GROUPS)]
    for i in range(per_step):
        b = pl.program_id(0) * per_step + i
        is_b = lane == b
        x_cols = [jnp.sum(jnp.where(is_b, xdt_t_ref[rows, :], 0.0), axis=1, keepdims=True) for rows in groups]
        y_cols = []
        for g, rows in enumerate(groups):
            pieces = []
            for hh in range(hpg):
                h = g * hpg + hh
                rs = slice(h * SSM_HEAD_DIM, (h + 1) * SSM_HEAD_DIM)
                pieces.append(st_ref[i, rs, :] * dec_ref[b, h])
            new = jnp.concatenate(pieces, axis=0) + x_cols[g] * bm_ref[i, g:g + 1, :]
            st_out_ref[i, rows, :] = new
            y_cols.append(jnp.sum(new * cm_ref[i, g:g + 1, :], axis=1, keepdims=True))
        for rows, y_col in zip(groups, y_cols):
            y_t[rows, :] = jnp.where(is_b, y_col, y_t[rows, :])

    @pl.when(pl.program_id(0) == pl.num_programs(0) - 1)
    def _():
        for g in range(SSM_GROUPS):
            cols = slice(g * gw, (g + 1) * gw)
            y = y_t[cols, :].T + dskip_ref[:, cols] * xs_ref[:, cols]
            y = y * _silu(z_ref[:, cols])
            y_ref[:, cols] = (_rms_scale(y) * gssm_ref[:, cols]).astype(BF16)


def _ssd_sample_state(dec, state, xdt_t, bm, cm, xs, z, dskip, gssm, *, per_step=4):
    nb = state.shape[0]
    per_step = math.gcd(per_step, nb)
    grid_spec = pl.GridSpec(
        grid=(nb // per_step,),
        in_specs=[pl.BlockSpec(memory_space=pltpu.SMEM),
                  pl.BlockSpec((per_step, D_INNER, D_STATE), lambda i: (i, 0, 0)),
                  _const_spec(xdt_t.shape),
                  pl.BlockSpec((per_step, SSM_GROUPS, D_STATE), lambda i: (i, 0, 0)),
                  pl.BlockSpec((per_step, SSM_GROUPS, D_STATE), lambda i: (i, 0, 0)),
                  _const_spec(xs.shape), _const_spec(z.shape), _const_spec(dskip.shape), _const_spec(gssm.shape)],
        out_specs=[pl.BlockSpec((per_step, D_INNER, D_STATE), lambda i: (i, 0, 0)),
                   pl.BlockSpec((nb, D_INNER), lambda i: (0, 0))],
        scratch_shapes=[pltpu.VMEM((D_INNER, nb), F32)])
    return pl.pallas_call(
        _ssd_sample_state_kernel, grid_spec=grid_spec,
        out_shape=[jax.ShapeDtypeStruct(state.shape, F32), jax.ShapeDtypeStruct((nb, D_INNER), BF16)],
        compiler_params=_params(1), name="ssd_sample_state")(
            dec, state, xdt_t, bm.reshape(nb, SSM_GROUPS, D_STATE), cm.reshape(nb, SSM_GROUPS, D_STATE),
            xs, z, dskip, gssm)


def _post_kernel(x_ref, oa_ref, os_ref, sga_ref, sgb_ref, wa_ref, ws_ref, wo_ref, gpm_ref, gpre_ref,
                 wup_ref, wdn_ref, gpost_ref, y_ref, *, ff_chunk):
    merged = sga_ref[...] * _dot(oa_ref[...], wa_ref[...]) + sgb_ref[...] * _dot(os_ref[...], ws_ref[...])
    mix = _dot(merged.astype(BF16), wo_ref[...])
    x1 = x_ref[...] + _rms_scale(mix) * gpm_ref[...]
    h2 = (_rms_scale(x1) * gpre_ref[...]).astype(BF16)
    f = None
    for c in range(D_FF // ff_chunk):
        cs = slice(c * ff_chunk, (c + 1) * ff_chunk)
        u = jnp.square(jnp.maximum(_dot(h2, wup_ref[:, cs]), 0.0)).astype(BF16)
        d = _dot(u, wdn_ref[cs, :])
        f = d if f is None else f + d
    y_ref[...] = x1 + _rms_scale(f) * gpost_ref[...]


def _post(x, oa, os_, sga, sgb, wa, ws, wo, gpm, gpre, wup, wdn, gpost, *, tm, name, ff_chunk=1024):
    n = x.shape[0]
    row = lambda w: pl.BlockSpec((tm, w), lambda i: (i, 0))
    consts = (wa, ws, wo, gpm, gpre, wup, wdn, gpost)
    return pl.pallas_call(
        functools.partial(_post_kernel, ff_chunk=ff_chunk), grid=(n // tm,),
        in_specs=[row(D_MODEL), row(MLA_HEADS * V_DIM), row(D_INNER), row(D_MODEL), row(D_MODEL)]
        + [_const_spec(a.shape) for a in consts],
        out_specs=row(D_MODEL),
        out_shape=jax.ShapeDtypeStruct((n, D_MODEL), F32),
        compiler_params=_params(1), name=name)(x, oa, os_, sga, sgb, *consts)


def _rope_tables(pos):
    half = ROPE_DIM // 2
    inv_freq = ROPE_THETA ** (-jnp.arange(half, dtype=F32) / half)
    ang = pos.astype(F32)[:, None] * inv_freq[None, :]
    cos, sin = jnp.cos(ang), jnp.sin(ang)
    n = pos.shape[0]
    ones_lo = jnp.ones((n, NOPE_DIM), F32)
    pad_hi = jnp.ones((n, HEAD_PAD - NOPE_DIM - ROPE_DIM), F32)
    zeros = lambda w: jnp.zeros((n, w), F32)
    c = jnp.concatenate([ones_lo, cos, cos, pad_hi], axis=1)
    s1 = jnp.concatenate([zeros(NOPE_DIM), -sin, zeros(HEAD_PAD - NOPE_DIM - half)], axis=1)
    s2 = jnp.concatenate([zeros(NOPE_DIM + half), sin, zeros(HEAD_PAD - NOPE_DIM - ROPE_DIM)], axis=1)
    return c, s1, s2


def _pad_cols(w, lo, total):
    return jnp.pad(w, ((0, 0), (lo, total - lo - w.shape[1])))


def _row(v):
    return v.reshape(1, -1)


def kernel(x_prompt, x_sample, cache_kv_latent, cache_k_rope, state_ssm, state_conv, page_table, w_in, g_pre_mix, g_q, w_uq, g_kv, w_uk, w_uv, w_conv, b_conv, dt_bias, a_log, d_skip, g_ssm, w_br_attn, w_br_ssm, w_out, g_post_mix, g_pre_mlp, w_up, w_down, g_post_mlp):
    depth = w_in.shape[0]
    assert depth == 1
    bp, sp, _ = x_prompt.shape
    bs, ss, _ = x_sample.shape
    assert ss == 1
    past = page_table.shape[1] * PAGE_SIZE
    l = 0

    o_q, o_kv, o_pe = 0, Q_LORA, Q_LORA + KV_LORA
    o_z = o_pe + ROPE_DIM
    o_xbc = o_z + D_INNER
    o_dt = o_xbc + CONV_DIM
    o_ga = o_dt + SSM_HEADS
    o_gb = o_ga + D_MODEL
    wi = w_in[l]
    wlat = jnp.concatenate([wi[:, o_q:o_pe], _pad_cols(wi[:, o_pe:o_z], NOPE_DIM, HEAD_PAD)], axis=1).astype(BF16)
    wssm = jnp.concatenate([wi[:, o_z:o_dt], _pad_cols(wi[:, o_dt:o_ga], 0, LANES), wi[:, o_ga:]], axis=1).astype(BF16)
    wuq = jnp.pad(w_uq[l], ((0, 0), (0, 0), (0, HEAD_PAD - NOPE_DIM - ROPE_DIM))).reshape(Q_LORA, -1).astype(BF16)
    wuk = jnp.pad(w_uk[l], ((0, 0), (0, 0), (0, HEAD_PAD - NOPE_DIM))).reshape(KV_LORA, -1).astype(BF16)
    wuv = w_uv[l].reshape(KV_LORA, -1).astype(BF16)
    wv4 = w_uv[l].reshape(KV_LORA, MLA_HEADS // 2, 2, V_DIM)
    zv = jnp.zeros_like(wv4[:, :, 0])
    wuv_aug = jnp.stack([wv4[:, :, 0], zv, zv, wv4[:, :, 1]], axis=2).reshape(KV_LORA, -1).astype(BF16)
    v_ones = jnp.tile(jnp.concatenate([jnp.zeros((V_DIM,), F32), jnp.ones((2 * V_DIM,), F32),
                                       jnp.zeros((V_DIM,), F32)]), MLA_HEADS // 2).reshape(1, -1)
    gpre, gq, gkv = _row(g_pre_mix[l]), _row(g_q[l]), _row(g_kv[l])
    wconv, bconv = w_conv[l], _row(b_conv[l])
    dtb = _pad_cols(_row(dt_bias[l]), 0, LANES)
    alog = _pad_cols(_row(a_log[l]), 0, LANES)
    dskip = _row(jnp.repeat(d_skip[l], SSM_HEAD_DIM))
    gssm = _row(g_ssm[l])
    post_w = (w_br_attn[l].astype(BF16), w_br_ssm[l].astype(BF16), w_out[l].astype(BF16), _row(g_post_mix[l]),
              _row(g_pre_mlp[l]), w_up[l].astype(BF16), w_down[l].astype(BF16), _row(g_post_mlp[l]))

    xp = x_prompt.reshape(bp * sp, D_MODEL)
    tm_p = min(512, sp)
    tabs_p = _rope_tables(jnp.arange(sp, dtype=jnp.int32))
    qh_p, ckv_p, kpe_p, kh_p, v_p = _mla_proj(xp, gpre, wlat, gq, wuq, gkv, tabs_p, wuk, (wuv_aug, v_ones),
                                              prompt=True, tm=tm_p, tab_blocks=sp // tm_p)
    z_p, xsc_p, bc_p, dt_p, sga_p, sgb_p, conv_p = _ssm_proj(xp, gpre, wssm, wconv, bconv, tm=min(256, sp), seq=sp,
                                                             name="ssm_proj_prompt")
    oa_p = _attn_prompt(qh_p.reshape(bp, sp, -1), kh_p.reshape(bp, sp, -1), v_p.reshape(bp, sp, -1),
                        tq=min(256, sp))
    os_p, ssm_p = _ssd_prompt(xsc_p, bc_p, z_p, dt_p, dtb, alog, dskip, gssm, batch=bp, seq=sp)
    y_p = _post(xp, oa_p.reshape(bp * sp, -1), os_p, sga_p, sgb_p, *post_w, tm=min(256, sp), name="post_prompt")

    xs_in = x_sample.reshape(bs, D_MODEL)
    tabs_s = _rope_tables(jnp.full((bs,), past, dtype=jnp.int32))
    qh_s, ckv_s, kpe_s, qlat_s = _mla_proj(xs_in, gpre, wlat, gq, wuq, gkv, tabs_s, wuk, None,
                                           prompt=False, tm=bs, tab_blocks=1)
    z_s, xbc_s, dt_s, sga_s, sgb_s = _ssm_proj(xs_in, gpre, wssm, tm=bs, name="ssm_proj_sample")
    o_lat = _attn_sample(page_table, qlat_s, qh_s, ckv_s, kpe_s, cache_kv_latent[l],
                         jnp.swapaxes(cache_k_rope[l], 1, 2))
    oa_s = _uv_project(o_lat.reshape(bs, MLA_HEADS * KV_LORA), wuv)
    cnew_s, xs_s, xdt_t, bm_s, cm_s, dec_s = _ssd_sample_pre(
        xbc_s, state_conv[l].reshape(bs, (CONV_W - 1) * CONV_DIM), dt_s, wconv, bconv, dtb, alog)
    ssm_s, os_s = _ssd_sample_state(dec_s[:, :SSM_HEADS], state_ssm[l].reshape(bs, D_INNER, D_STATE),
                                    xdt_t, bm_s, cm_s, xs_s, z_s, dskip, gssm)
    y_s = _post(xs_in, oa_s, os_s, sga_s, sgb_s, *post_w, tm=bs, name="post_sample")

    hshape = (SSM_HEADS, SSM_HEAD_DIM, D_STATE)
    return (y_p.reshape(bp, sp, D_MODEL), y_s.reshape(bs, 1, D_MODEL),
            ckv_p.reshape(1, bp, sp, KV_LORA), kpe_p.reshape(1, bp, sp, ROPE_DIM),
            ssm_p.reshape(1, bp, *hshape), conv_p.reshape(1, bp, CONV_W - 1, CONV_DIM),
            ckv_s.reshape(1, bs, 1, KV_LORA), kpe_s.reshape(1, bs, 1, ROPE_DIM),
            ssm_s.reshape(1, bs, *hshape), cnew_s.reshape(1, bs, CONV_W - 1, CONV_DIM))
```

```python
import functools
import math

import jax
import jax.numpy as jnp
from jax import lax
from jax.experimental import pallas as pl
from jax.experimental.pallas import tpu as pltpu

F32 = jnp.float32
BF16 = jnp.bfloat16

D_MODEL = 1024
MLA_HEADS = 16
Q_LORA = 384
KV_LORA = 256
NOPE_DIM = 64
ROPE_DIM = 32
V_DIM = 64
ROPE_THETA = 10000.0
SM_SCALE = 1.0 / math.sqrt(NOPE_DIM + ROPE_DIM)
_LOG2E = math.log2(math.e)
_EXP2_SCALE = SM_SCALE * _LOG2E
PAGE_SIZE = 128
D_INNER = 2048
SSM_HEAD_DIM = 64
SSM_HEADS = D_INNER // SSM_HEAD_DIM
SSM_GROUPS = 8
D_STATE = 128
CONV_W = 4
CONV_DIM = D_INNER + 2 * SSM_GROUPS * D_STATE
SSD_CHUNK = 128
D_FF = 4096
EPS = 1e-6

LANES = 128
SUBLANES = 8
HEAD_PAD = LANES
VMEM_LIMIT = 56 * 1024 * 1024

_NT = (((1,), (1,)), ((), ()))
_TN = (((0,), (0,)), ((), ()))


def _rms_scale(x):
    return x * lax.rsqrt(jnp.mean(x * x, axis=-1, keepdims=True) + EPS)


def _sigmoid(x):
    return 1.0 / (1.0 + jnp.exp(-x))


def _silu(x):
    return x * _sigmoid(x)


def _softplus(x):
    return jnp.maximum(x, 0.0) + jnp.log(1.0 + jnp.exp(-jnp.abs(x)))


def _dot(a, b):
    return jnp.dot(a, b, preferred_element_type=F32)


def _const_spec(shape):
    nd = len(shape)
    return pl.BlockSpec(shape, lambda *_: (0,) * nd, pipeline_mode=pl.Buffered(1))


def _params(n_grid, vmem=VMEM_LIMIT, flags=None):
    return pltpu.CompilerParams(dimension_semantics=("arbitrary",) * n_grid, vmem_limit_bytes=vmem, flags=flags)


def _rope_tile(t, c, s1, s2):
    return t * c + pltpu.roll(t, LANES - ROPE_DIM // 2, 1) * s1 + pltpu.roll(t, ROPE_DIM // 2, 1) * s2


def _mla_proj_kernel(*refs, prompt):
    (x_ref, gpre_ref, wlat_ref, gq_ref, wuq_ref, gkv_ref, rc_ref, rs1_ref, rs2_ref, wuk_ref) = refs[:10]
    if prompt:
        wuv_ref, vone_ref, qh_ref, ckv_ref, kpe_ref, kh_ref, v_ref = refs[10:]
    else:
        qh_ref, ckv_ref, kpe_ref, qlat_ref = refs[10:]
    h = (_rms_scale(x_ref[...]) * gpre_ref[...]).astype(BF16)
    lat = _dot(h, wlat_ref[...])
    cq = (_rms_scale(lat[:, :Q_LORA]) * gq_ref[...]).astype(BF16)
    ckv = _rms_scale(lat[:, Q_LORA:Q_LORA + KV_LORA]) * gkv_ref[...]
    ckv_ref[...] = ckv
    c, s1, s2 = rc_ref[...], rs1_ref[...], rs2_ref[...]
    kpe = _rope_tile(lat[:, Q_LORA + KV_LORA:], c, s1, s2)
    if prompt:
        kpe_ref[0] = kpe.T[NOPE_DIM:NOPE_DIM + ROPE_DIM, :]
    else:
        kpe_ref[...] = kpe[:, NOPE_DIM:NOPE_DIM + ROPE_DIM]
    ckv_b = ckv.astype(BF16)
    for hp in range(MLA_HEADS // 2):
        cols = slice(2 * hp * HEAD_PAD, (2 * hp + 2) * HEAD_PAD)
        q2 = _dot(cq, wuq_ref[:, cols])
        if prompt:
            k2 = _dot(ckv_b, wuk_ref[:, cols])
            v_ref[:, cols] = (_dot(ckv_b, wuv_ref[:, cols]) + vone_ref[:, cols]).astype(BF16)
        for j in range(2):
            sl = slice((2 * hp + j) * HEAD_PAD, (2 * hp + j + 1) * HEAD_PAD)
            loc = slice(j * HEAD_PAD, (j + 1) * HEAD_PAD)
            qh = _rope_tile(q2[:, loc], c, s1, s2)
            if prompt:
                qh = qh * _EXP2_SCALE
            qh = qh.astype(BF16)
            qh_ref[:, sl] = qh
            if prompt:
                kh_ref[:, sl] = (k2[:, loc] + kpe).astype(BF16)
            else:
                hd = 2 * hp + j
                qlat_ref[:, hd * KV_LORA:(hd + 1) * KV_LORA] = lax.dot_general(
                    qh, wuk_ref[:, sl], _NT, preferred_element_type=F32)


def _mla_proj(x, gpre, wlat, gq, wuq, gkv, rope_tabs, wuk, wuv, *, prompt, tm, tab_blocks):
    n = x.shape[0]
    hp = MLA_HEADS * HEAD_PAD
    row = lambda w: pl.BlockSpec((tm, w), lambda i: (i, 0))
    tab = pl.BlockSpec((tm, LANES), lambda i: (i % tab_blocks, 0))
    in_specs = [row(D_MODEL), _const_spec(gpre.shape), _const_spec(wlat.shape), _const_spec(gq.shape),
                _const_spec(wuq.shape), _const_spec(gkv.shape), tab, tab, tab, _const_spec(wuk.shape)]
    args = [x, gpre, wlat, gq, wuq, gkv, *rope_tabs, wuk]
    out_shape = [jax.ShapeDtypeStruct((n, hp), BF16), jax.ShapeDtypeStruct((n, KV_LORA), F32),
                 jax.ShapeDtypeStruct((n, ROPE_DIM), F32)]
    out_specs = [row(hp), row(KV_LORA), row(ROPE_DIM)]
    if prompt:
        out_shape[2] = jax.ShapeDtypeStruct((n // (tm * tab_blocks), ROPE_DIM, tm * tab_blocks), F32)
        out_specs[2] = pl.BlockSpec((1, ROPE_DIM, tm), lambda i: (i // tab_blocks, 0, i % tab_blocks))
        wuv_aug, v_ones = wuv
        in_specs += [_const_spec(wuv_aug.shape), _const_spec(v_ones.shape)]
        args += [wuv_aug, v_ones]
        out_shape += [jax.ShapeDtypeStruct((n, hp), BF16), jax.ShapeDtypeStruct((n, hp), BF16)]
        out_specs += [row(hp), row(hp)]
    else:
        out_shape.append(jax.ShapeDtypeStruct((n, MLA_HEADS * KV_LORA), F32))
        out_specs.append(row(MLA_HEADS * KV_LORA))
    return pl.pallas_call(
        functools.partial(_mla_proj_kernel, prompt=prompt),
        grid=(n // tm,), in_specs=in_specs, out_specs=out_specs, out_shape=out_shape,
        compiler_params=_params(1), name="mla_proj_prompt" if prompt else "mla_proj_sample")(*args)


_CONV_TAIL = CONV_W - 1
_CONV_COLS = 256


def _ssm_proj_kernel(*refs, conv, blocks_per_seq):
    x_ref, gpre_ref, wz_ref, wxbc_ref, wdt_ref, wga_ref, wgb_ref = refs[:7]
    if conv:
        (wconv_ref, bconv_ref, z_ref, xs_ref, bc_ref, dt_ref, sga_ref, sgb_ref, tail_ref, h_scr) = refs[7:17]
        xpads = refs[17:]
    else:
        z_ref, xbc_ref, dt_ref, sga_ref, sgb_ref, h_scr = refs[7:]
    tm = x_ref.shape[0]
    h_scr[...] = (_rms_scale(x_ref[...]) * gpre_ref[...]).astype(BF16)

    def project(w_ref, lo, width):
        return _dot(h_scr[...], w_ref[:, lo:lo + width])

    def plain(out_ref, w_ref, act):
        width = w_ref.shape[1]

        def run(lo):
            w = min(_CONV_COLS, width - lo)
            out_ref[:, lo:lo + w] = act(project(w_ref, lo, w))
        return [functools.partial(run, lo) for lo in range(0, width, _CONV_COLS)]

    ident = lambda v: v
    others = (plain(z_ref, wz_ref, ident) + plain(sga_ref, wga_ref, _sigmoid)
              + plain(sgb_ref, wgb_ref, _sigmoid) + plain(dt_ref, wdt_ref, ident))
    if not conv:
        xbc_ref[...] = project(wxbc_ref, 0, CONV_DIM)
        for f in others:
            f()
        return

    first = pl.program_id(0) % blocks_per_seq == 0
    n_chunks = CONV_DIM // _CONV_COLS
    n_tiles = tm // SUBLANES

    @pl.when(first)
    def _():
        for xp in xpads:
            xp[0:SUBLANES, :] = jnp.zeros((SUBLANES, _CONV_COLS), F32)

    @pl.when(jnp.logical_not(first))
    def _():
        for xp in xpads:
            xp[0:SUBLANES, :] = xp[tm:tm + SUBLANES, :]

    sub = lax.broadcasted_iota(jnp.int32, (n_tiles, SUBLANES, _CONV_COLS), 1)

    def project_chunk(i):
        xpads[i][SUBLANES:, :] = project(wxbc_ref, i * _CONV_COLS, _CONV_COLS)

    project_chunk(0)
    for i in range(n_chunks):
        lo = i * _CONV_COLS
        cols = slice(lo, lo + _CONV_COLS)
        if i + 1 < n_chunks:
            project_chunk(i + 1)
        if i < len(others):
            others[i]()
        tiles = xpads[i][...].reshape(n_tiles + 1, SUBLANES, _CONV_COLS)
        acc = bconv_ref[:, cols]
        for k in range(CONV_W):
            s = _CONV_TAIL - k
            if s == 0:
                shifted = tiles[1:]
            else:
                r = pltpu.roll(tiles, s, 1)
                shifted = jnp.where(sub >= s, r[1:], r[:-1])
            acc = acc + shifted.reshape(tm, _CONV_COLS) * wconv_ref[k:k + 1, cols]
        u = _silu(acc)
        if lo < D_INNER:
            xs_ref[:, cols] = u
        else:
            bc_ref[:, lo - D_INNER:lo - D_INNER + _CONV_COLS] = u.astype(BF16)
        tail_ref[0, :, cols] = xpads[i][SUBLANES + tm - _CONV_TAIL:SUBLANES + tm, :]
    for f in others[n_chunks:]:
        f()


def _ssm_proj(x, gpre, wssm, wconv=None, bconv=None, *, tm, seq=None, name):
    n = x.shape[0]
    conv = wconv is not None
    row = lambda w, dt=F32: (pl.BlockSpec((tm, w), lambda i: (i, 0)), jax.ShapeDtypeStruct((n, w), dt))
    in_specs = ([pl.BlockSpec((tm, D_MODEL), lambda i: (i, 0)), _const_spec(gpre.shape)]
                + [_const_spec(w.shape) for w in wssm])
    args = [x, gpre, *wssm]
    scratch = [pltpu.VMEM((tm, D_MODEL), BF16)]
    if conv:
        bps = seq // tm
        in_specs += [_const_spec(wconv.shape), _const_spec(bconv.shape)]
        args += [wconv, bconv]
        outs = [row(D_INNER), row(D_INNER), row(CONV_DIM - D_INNER, BF16), row(LANES), row(D_MODEL), row(D_MODEL),
                (pl.BlockSpec((1, _CONV_TAIL, CONV_DIM), lambda i: (i // bps, 0, 0)),
                 jax.ShapeDtypeStruct((n // seq, _CONV_TAIL, CONV_DIM), F32))]
        scratch += [pltpu.VMEM((SUBLANES + tm, _CONV_COLS), F32)] * (CONV_DIM // _CONV_COLS)
    else:
        bps = 1
        outs = [row(D_INNER), row(CONV_DIM), row(LANES), row(D_MODEL), row(D_MODEL)]
    return pl.pallas_call(
        functools.partial(_ssm_proj_kernel, conv=conv, blocks_per_seq=bps), grid=(n // tm,),
        in_specs=in_specs, out_specs=[o[0] for o in outs], out_shape=[o[1] for o in outs],
        scratch_shapes=scratch, compiler_params=_params(1), name=name)(*args)


def _fold_lanes(x, op):
    out = x[:, :LANES]
    for c in range(1, x.shape[1] // LANES):
        out = op(out, x[:, c * LANES:(c + 1) * LANES])
    return out


def _attn_prompt_kernel(q_ref, k_ref, v_ref, o_ref, s_scr, p_scr, o_scr, *, seq, tq):
    lane = lax.broadcasted_iota(jnp.int32, (tq, LANES), 1)
    row = lax.broadcasted_iota(jnp.int32, (tq, tq), 0)
    col = lax.broadcasted_iota(jnp.int32, (tq, tq), 1)
    causal = col <= row
    chains = [(qi, j) for qi in range(seq // tq) for j in range(2)]
    row_max = {}

    def scores(c):
        qi, j = chains[c]
        hs = slice(j * HEAD_PAD, (j + 1) * HEAD_PAD)
        q = q_ref[0, qi * tq:(qi + 1) * tq, hs]
        mx = None
        for kt in range(qi + 1):
            ks = slice(kt * tq, (kt + 1) * tq)
            s = lax.dot_general(q, k_ref[0, ks, hs], _NT, preferred_element_type=F32)
            if kt == qi:
                s = jnp.where(causal, s, -jnp.inf)
            s_scr[c % 2, :, ks] = s
            m2 = _fold_lanes(s, jnp.maximum)
            mx = m2 if mx is None else jnp.maximum(mx, m2)
        row_max[c] = jnp.max(mx, axis=-1, keepdims=True)

    def numerators(c):
        qi, _ = chains[c]
        m = row_max.pop(c)
        for kt in range(qi + 1):
            ks = slice(kt * tq, (kt + 1) * tq)
            p_scr[c % 2, :, ks] = jnp.exp2(s_scr[c % 2, :, ks] - m).astype(BF16)

    def weighted_values(c):
        qi, j = chains[c]
        lo, hi = qi * tq, (qi + 1) * tq
        acc = _dot(p_scr[c % 2, :, :hi], v_ref[0, :hi, j * LANES:(j + 1) * LANES])
        out = acc / pltpu.roll(acc, V_DIM, 1)
        if j == 0:
            o_scr[...] = out
        else:
            o_ref[0, lo:hi, :] = jnp.where(lane < V_DIM, o_scr[...], out).astype(BF16)

    n = len(chains)
    for step in range(n + 2):
        if step < n:
            scores(step)
        if 1 <= step <= n:
            numerators(step - 1)
        if step >= 2:
            weighted_values(step - 2)


def _attn_prompt(qh, kh, v, *, tq=256):
    b, s, _ = qh.shape
    pair = 2 * HEAD_PAD
    return pl.pallas_call(
        functools.partial(_attn_prompt_kernel, seq=s, tq=tq),
        grid=(b, MLA_HEADS // 2),
        in_specs=[pl.BlockSpec((1, s, pair), lambda i, j: (i, 0, j)),
                  pl.BlockSpec((1, s, pair), lambda i, j: (i, 0, j)),
                  pl.BlockSpec((1, s, pair), lambda i, j: (i, 0, j))],
        out_specs=pl.BlockSpec((1, s, 2 * V_DIM), lambda i, j: (i, 0, j)),
        out_shape=jax.ShapeDtypeStruct((b, s, MLA_HEADS * V_DIM), BF16),
        scratch_shapes=[pltpu.VMEM((2, tq, s), F32), pltpu.VMEM((2, tq, s), BF16),
                        pltpu.VMEM((tq, LANES), F32)],
        compiler_params=_params(2), name="attn_prompt")(qh, kh, v)


def _attn_sample_kernel(pt_ref, qlat_ref, qh_ref, ckv_ref, kpe_ref, ckv_prev_ref, ckv_hbm, cpet_hbm, o_ref,
                        kv0, kv1, pe0, pe1, kvb0, kvb1, s0, s1, st0, st1, sem, *, n_pages, tile, nb):
    g = pl.program_id(0)
    past = n_pages * PAGE_SIZE
    n_tiles = past // tile
    kv, pe, kvb, sc_scr, st = (kv0, kv1), (pe0, pe1), (kvb0, kvb1), (s0, s1), (st0, st1)

    def start_fetch(seq, r):
        for j in range(n_pages):
            pg = pt_ref[seq, j]
            rows = pl.ds(j * PAGE_SIZE, PAGE_SIZE)
            pltpu.make_async_copy(ckv_hbm.at[pg], kv[r].at[rows], sem.at[0, r]).start()
            pltpu.make_async_copy(cpet_hbm.at[pg], pe[r].at[:, rows], sem.at[1, r]).start()

    def wait_fetch(r):
        pltpu.make_async_copy(kv[r], kv[r], sem.at[0, r]).wait()
        pltpu.make_async_copy(pe[r], pe[r], sem.at[1, r]).wait()

    @pl.when(g == 0)
    def _():
        start_fetch(0, 0)
        kvb1[...] = jnp.zeros(kvb1.shape, BF16)
        s1[...] = jnp.zeros(s1.shape, F32)
        st1[...] = jnp.zeros(st1.shape, F32)

    def half_step(r):
        h = 2 * g + r
        p_ = 1 - r
        start_fetch(jnp.minimum(h + 1, nb - 1), p_)
        wait_fetch(r)

        qlat = qlat_ref[r]
        qlat_b = qlat.astype(BF16)
        qpe_b = qh_ref[r][:, NOPE_DIM:NOPE_DIM + ROPE_DIM]
        m_prev = st[p_][0][:, :1]
        snew_prev = st[p_][1][:, :1]
        ckv_prev = ckv_prev_ref[r]
        p_new = jnp.exp2((snew_prev - m_prev) * _EXP2_SCALE)
        psum = None
        accs = [p_new * ckv_prev, None]
        mx = None
        for t in range(n_tiles):
            ts = slice(t * tile, (t + 1) * tile)
            kc = kv[r][ts, :].astype(BF16)
            kvb[r][ts, :] = kc
            sc = (lax.dot_general(qlat_b, kc, _NT, preferred_element_type=F32)
                  + _dot(qpe_b, pe[r][:, ts].astype(BF16)))
            sc_scr[r][:, ts] = sc
            mx = sc if mx is None else jnp.maximum(mx, sc)
            p = jnp.exp2((sc_scr[p_][:, ts] - m_prev) * _EXP2_SCALE)
            psum = p if psum is None else psum + p
            d = _dot(p.astype(BF16), kvb[p_][ts, :])
            accs[t % 2] = d if accs[t % 2] is None else accs[t % 2] + d
        ckv_new = ckv_ref[r]
        kpe_new = kpe_ref[r]
        s_new = (jnp.sum(qlat * ckv_new, axis=-1, keepdims=True)
                 + jnp.sum(qpe_b.astype(F32) * kpe_new, axis=-1, keepdims=True))
        m = jnp.maximum(jnp.max(mx, axis=-1, keepdims=True), s_new)
        st[r][0] = jnp.broadcast_to(m, (MLA_HEADS, LANES))
        st[r][1] = jnp.broadcast_to(s_new, (MLA_HEADS, LANES))
        den = p_new + jnp.sum(psum, axis=-1, keepdims=True)
        acc = accs[0] if accs[1] is None else accs[0] + accs[1]
        o_ref[r] = acc / den

    half_step(0)
    half_step(1)

    @pl.when(g == pl.num_programs(0) - 1)
    def _():
        wait_fetch(0)


def _attn_sample(page_table, qlat, qh, ckv, kpe, cache_kv, cache_pe_t, *, tile=256):
    nb, n_pages = page_table.shape
    assert nb % 2 == 0
    past = n_pages * PAGE_SIZE
    last = nb // 2 - 1
    cur = lambda i, pt: (jnp.minimum(i, last), 0, 0)
    pair = lambda w: pl.BlockSpec((2, MLA_HEADS, w), cur)
    vec = lambda w: pl.BlockSpec((2, 1, w), cur)
    slot = lambda shape, dt: [pltpu.VMEM(shape, dt)] * 2
    grid_spec = pltpu.PrefetchScalarGridSpec(
        num_scalar_prefetch=1, grid=(nb // 2 + 1,),
        in_specs=[pair(KV_LORA), pair(HEAD_PAD), vec(KV_LORA), vec(ROPE_DIM),
                  pl.BlockSpec((2, 1, KV_LORA), lambda i, pt: (i, 0, 0)),
                  pl.BlockSpec(memory_space=pl.ANY),
                  pl.BlockSpec(memory_space=pl.ANY)],
        out_specs=pl.BlockSpec((2, MLA_HEADS, KV_LORA), lambda i, pt: (i, 0, 0)),
        scratch_shapes=(slot((past, KV_LORA), F32) + slot((ROPE_DIM, past), F32) + slot((past, KV_LORA), BF16)
                        + slot((MLA_HEADS, past), F32) + slot((2, MLA_HEADS, LANES), F32)
                        + [pltpu.SemaphoreType.DMA((2, 2))]))
    ckv3 = ckv.reshape(nb, 1, KV_LORA)
    out = pl.pallas_call(
        functools.partial(_attn_sample_kernel, n_pages=n_pages, tile=min(tile, past), nb=nb),
        grid_spec=grid_spec,
        out_shape=jax.ShapeDtypeStruct((nb + 2, MLA_HEADS, KV_LORA), F32),
        compiler_params=_params(1), name="attn_sample")(
            page_table, qlat.reshape(nb, MLA_HEADS, KV_LORA), qh.reshape(nb, MLA_HEADS, HEAD_PAD),
            ckv3, kpe.reshape(nb, 1, ROPE_DIM), jnp.pad(ckv3, ((1, 1), (0, 0), (0, 0))), cache_kv, cache_pe_t)
    return out[1:nb + 1]


def _uv_kernel(ol_ref, wuv_ref, o_ref):
    lane = lax.broadcasted_iota(jnp.int32, o_ref.shape[:1] + (LANES,), 1)
    for hp in range(MLA_HEADS // 2):
        w = wuv_ref[:, hp * LANES:(hp + 1) * LANES]
        a0 = _dot(ol_ref[:, (2 * hp) * KV_LORA:(2 * hp + 1) * KV_LORA].astype(BF16), w)
        a1 = _dot(ol_ref[:, (2 * hp + 1) * KV_LORA:(2 * hp + 2) * KV_LORA].astype(BF16), w)
        o_ref[:, hp * LANES:(hp + 1) * LANES] = jnp.where(lane < V_DIM, a0, a1).astype(BF16)


def _uv_project(o_lat, wuv):
    n = o_lat.shape[0]
    return pl.pallas_call(
        _uv_kernel, grid=(1,),
        in_specs=[_const_spec(o_lat.shape), _const_spec(wuv.shape)],
        out_specs=pl.BlockSpec((n, MLA_HEADS * V_DIM), lambda i: (0, 0)),
        out_shape=jax.ShapeDtypeStruct((n, MLA_HEADS * V_DIM), BF16),
        compiler_params=_params(1), name="uv_project")(o_lat, wuv)


def _cumsum_rows(v):
    rowi = lax.broadcasted_iota(jnp.int32, v.shape, 0)
    k = 1
    while k < v.shape[0]:
        v = v + jnp.where(rowi >= k, pltpu.roll(v, k, 0), 0.0)
        k *= 2
    return v


def _pair_cols(mat, h0, lane):
    t = mat.shape[0]
    a = jnp.broadcast_to(mat[:, h0:h0 + 1], (t, LANES))
    b = jnp.broadcast_to(mat[:, h0 + 1:h0 + 2], (t, LANES))
    return jnp.where(lane < SSM_HEAD_DIM, a, b)


def _expand_heads(x, e_ref, lane):
    hi = x.astype(BF16).astype(F32)
    r1 = x - hi
    mid = r1.astype(BF16).astype(F32)
    lo = r1 - mid
    packed = jnp.where(lane < SSM_HEADS, hi,
                       jnp.where(lane < 2 * SSM_HEADS, pltpu.roll(mid, SSM_HEADS, 1),
                                 jnp.where(lane < 3 * SSM_HEADS, pltpu.roll(lo, 2 * SSM_HEADS, 1), 0.0)))
    return _dot(packed.astype(BF16), e_ref[...])


def _head_expander(width):
    r = jnp.arange(LANES)[:, None]
    c = jnp.arange(SSM_HEADS * width)[None, :]
    return ((r < 3 * SSM_HEADS) & (r % SSM_HEADS == c // width)).astype(BF16)


def _ssd_prompt_kernel(xs_ref, bc_ref, z_ref, dt_ref, dtb_ref, alog_ref, dskip_ref, gssm_ref, e128_ref, e64_ref,
                       y_ref, state_ref, state):
    t = SSD_CHUNK
    c = pl.program_id(1)

    @pl.when(c == 0)
    def _():
        state[...] = jnp.zeros(state.shape, F32)

    dt = _softplus(dt_ref[...] + dtb_ref[...])
    a = -jnp.exp(alog_ref[...])
    cs = _cumsum_rows(dt * a) * _LOG2E
    cs_last = cs[t - 1:t, :]
    ecs = jnp.exp2(cs)
    wloc = dt * jnp.exp2(cs_last - cs)
    cs_t = cs.T
    dt_t = dt.T
    cdec = jnp.broadcast_to(jnp.exp2(cs_t[:, t - 1:t]), (LANES, LANES))

    lane = lax.broadcasted_iota(jnp.int32, (t, LANES), 1)
    tri = lax.broadcasted_iota(jnp.int32, (t, t), 0) >= lax.broadcasted_iota(jnp.int32, (t, t), 1)
    hpg = SSM_HEADS // SSM_GROUPS
    gw = hpg * SSM_HEAD_DIM
    c_off = SSM_GROUPS * D_STATE
    cs_x = _expand_heads(cs, e128_ref, lane)
    ecs_x = _expand_heads(ecs, e64_ref, lane)
    wloc_x = _expand_heads(wloc, e64_ref, lane)

    for g in range(SSM_GROUPS):
        bm_b = bc_ref[:, g * D_STATE:(g + 1) * D_STATE]
        cm_b = bc_ref[:, c_off + g * D_STATE:c_off + (g + 1) * D_STATE]
        cb = lax.dot_general(cm_b, bm_b, _NT, preferred_element_type=F32)
        prev_b = state[g * gw:(g + 1) * gw, :].astype(BF16)
        yoff = lax.dot_general(cm_b, prev_b, _NT, preferred_element_type=F32)
        ys = []
        for pr in range(hpg // 2):
            h0 = g * hpg + 2 * pr
            x2 = xs_ref[:, h0 * SSM_HEAD_DIM:(h0 + 2) * SSM_HEAD_DIM]
            x2_b = x2.astype(BF16)
            yd = []
            for h in (h0, h0 + 1):
                seg = cs_x[:, h * LANES:(h + 1) * LANES] - cs_t[h:h + 1, :]
                dec = jnp.exp2(jnp.where(tri, seg, -jnp.inf))
                mh = (cb * dec * dt_t[h:h + 1, :]).astype(BF16)
                yd.append(_dot(mh, x2_b))
            pair = slice(h0 * SSM_HEAD_DIM, (h0 + 2) * SSM_HEAD_DIM)
            y2 = jnp.where(lane < SSM_HEAD_DIM, yd[0], yd[1])
            y2 = y2 + yoff[:, 2 * pr * SSM_HEAD_DIM:(2 * pr + 2) * SSM_HEAD_DIM] * ecs_x[:, pair]
            y2 = y2 + dskip_ref[:, pair] * x2
            ys.append(y2)
            xw = (x2 * wloc_x[:, pair]).astype(BF16)
            loc = lax.dot_general(xw, bm_b, _TN, preferred_element_type=F32)
            r0 = h0 * SSM_HEAD_DIM
            for k, h in enumerate((h0, h0 + 1)):
                rs = slice(r0 + k * SSM_HEAD_DIM, r0 + (k + 1) * SSM_HEAD_DIM)
                state[rs, :] = (state[rs, :] * cdec[h:h + 1, :]
                                + loc[k * SSM_HEAD_DIM:(k + 1) * SSM_HEAD_DIM, :])
        yg = jnp.concatenate(ys, axis=1) * _silu(z_ref[:, g * gw:(g + 1) * gw])
        y_ref[:, g * gw:(g + 1) * gw] = (_rms_scale(yg) * gssm_ref[:, g * gw:(g + 1) * gw]).astype(BF16)

    state_ref[0] = state[...]


def _ssd_prompt(xs, bc, z, dt, dtb, alog, dskip, gssm, *, batch, seq):
    nc = seq // SSD_CHUNK
    row = lambda w: pl.BlockSpec((SSD_CHUNK, w), lambda b, c: (b * nc + c, 0))
    e128, e64 = _head_expander(LANES), _head_expander(SSM_HEAD_DIM)
    return pl.pallas_call(
        _ssd_prompt_kernel, grid=(batch, nc),
        in_specs=[row(D_INNER), row(CONV_DIM - D_INNER), row(D_INNER), row(LANES)]
        + [_const_spec(a.shape) for a in (dtb, alog, dskip, gssm, e128, e64)],
        out_specs=[row(D_INNER), pl.BlockSpec((1, D_INNER, D_STATE), lambda b, c: (b, 0, 0))],
        out_shape=[jax.ShapeDtypeStruct((batch * seq, D_INNER), BF16),
                   jax.ShapeDtypeStruct((batch, D_INNER, D_STATE), F32)],
        scratch_shapes=[pltpu.VMEM((D_INNER, D_STATE), F32)],
        compiler_params=_params(2), name="ssd_prompt")(xs, bc, z, dt, dtb, alog, dskip, gssm, e128, e64)


def _ssd_sample_pre_kernel(xbc_ref, cprev_ref, dt_ref, wconv_ref, bconv_ref, dtb_ref, alog_ref,
                           cnew_ref, xs_ref, xdt_t_ref, bm_ref, cm_ref, dec_ref):
    tail = CONV_W - 1
    acc = bconv_ref[...]
    for k in range(tail):
        acc = acc + cprev_ref[k] * wconv_ref[k:k + 1, :]
    x_new = xbc_ref[...]
    acc = acc + x_new * wconv_ref[tail:tail + 1, :]
    u = _silu(acc)
    for k in range(1, tail):
        cnew_ref[k - 1] = cprev_ref[k]
    cnew_ref[tail - 1] = x_new
    xs = u[:, :D_INNER]
    xs_ref[...] = xs
    bm_ref[...] = u[:, D_INNER:D_INNER + SSM_GROUPS * D_STATE]
    cm_ref[...] = u[:, D_INNER + SSM_GROUPS * D_STATE:]
    dt = _softplus(dt_ref[...] + dtb_ref[...])
    dec_ref[...] = jnp.exp(dt * (-jnp.exp(alog_ref[...])))
    nb = xs.shape[0]
    lane = lax.broadcasted_iota(jnp.int32, (nb, LANES), 1)
    for pr in range(SSM_HEADS // 2):
        x2 = xs[:, pr * LANES:(pr + 1) * LANES] * _pair_cols(dt, 2 * pr, lane)
        xdt_t_ref[pr * LANES:(pr + 1) * LANES, :] = x2.T


def _ssd_sample_pre(xbc, cprev, dt, wconv, bconv, dtb, alog):
    nb = xbc.shape[0]
    ins = (xbc, cprev, dt, wconv, bconv, dtb, alog)
    shapes = [((CONV_W - 1, nb, CONV_DIM), F32), ((nb, D_INNER), F32), ((D_INNER, nb), F32),
              ((nb, SSM_GROUPS * D_STATE), F32), ((nb, SSM_GROUPS * D_STATE), F32), ((nb, LANES), F32)]
    return pl.pallas_call(
        _ssd_sample_pre_kernel, grid=(1,),
        in_specs=[_const_spec(a.shape) for a in ins],
        out_specs=[pl.BlockSpec(s, lambda i, nd=len(s): (0,) * nd) for s, _ in shapes],
        out_shape=[jax.ShapeDtypeStruct(s, d) for s, d in shapes],
        compiler_params=_params(1), name="ssd_sample_pre")(*ins)


def _ssd_sample_state_kernel(dec_ref, st_ref, xdt_t_ref, bm_ref, cm_ref, xs_ref, z_ref, dskip_ref, gssm_ref,
                             st_out_ref, y_ref, y_t):
    per_step = st_ref.shape[0]
    nb = xs_ref.shape[0]
    hpg = SSM_HEADS // SSM_GROUPS
    gw = hpg * SSM_HEAD_DIM
    lane = lax.broadcasted_iota(jnp.int32, (gw, nb), 1)
    groups = [slice(g * gw, (g + 1) * gw) for g in range(SSM_GROUPS)]
    for i in range(per_step):
        b = pl.program_id(0) * per_step + i
        is_b = lane == b
        x_cols = [jnp.sum(jnp.where(is_b, xdt_t_ref[rows, :], 0.0), axis=1, keepdims=True) for rows in groups]
        y_cols = []
        for g, rows in enumerate(groups):
            pieces = []
            for hh in range(hpg):
                h = g * hpg + hh
                rs = slice(h * SSM_HEAD_DIM, (h + 1) * SSM_HEAD_DIM)
                pieces.append(st_ref[i, rs, :] * dec_ref[b, h])
            new = jnp.concatenate(pieces, axis=0) + x_cols[g] * bm_ref[i, g:g + 1, :]
            st_out_ref[i, rows, :] = new
            y_cols.append(jnp.sum(new * cm_ref[i, g:g + 1, :], axis=1, keepdims=True))
        for rows, y_col in zip(groups, y_cols):
            y_t[rows, :] = jnp.where(is_b, y_col, y_t[rows, :])

    @pl.when(pl.program_id(0) == pl.num_programs(0) - 1)
    def _():
        for g in range(SSM_GROUPS):
            cols = slice(g * gw, (g + 1) * gw)
            y = y_t[cols, :].T + dskip_ref[:, cols] * xs_ref[:, cols]
            y = y * _silu(z_ref[:, cols])
            y_ref[:, cols] = (_rms_scale(y) * gssm_ref[:, cols]).astype(BF16)


def _ssd_sample_state(dec, state, xdt_t, bm, cm, xs, z, dskip, gssm, *, per_step=4):
    nb = state.shape[0]
    per_step = math.gcd(per_step, nb)
    grid_spec = pl.GridSpec(
        grid=(nb // per_step,),
        in_specs=[pl.BlockSpec(memory_space=pltpu.SMEM),
                  pl.BlockSpec((per_step, D_INNER, D_STATE), lambda i: (i, 0, 0)),
                  _const_spec(xdt_t.shape),
                  pl.BlockSpec((per_step, SSM_GROUPS, D_STATE), lambda i: (i, 0, 0)),
                  pl.BlockSpec((per_step, SSM_GROUPS, D_STATE), lambda i: (i, 0, 0)),
                  _const_spec(xs.shape), _const_spec(z.shape), _const_spec(dskip.shape), _const_spec(gssm.shape)],
        out_specs=[pl.BlockSpec((per_step, D_INNER, D_STATE), lambda i: (i, 0, 0)),
                   pl.BlockSpec((nb, D_INNER), lambda i: (0, 0))],
        scratch_shapes=[pltpu.VMEM((D_INNER, nb), F32)])
    return pl.pallas_call(
        _ssd_sample_state_kernel, grid_spec=grid_spec,
        out_shape=[jax.ShapeDtypeStruct(state.shape, F32), jax.ShapeDtypeStruct((nb, D_INNER), BF16)],
        compiler_params=_params(1), name="ssd_sample_state")(
            dec, state, xdt_t, bm.reshape(nb, SSM_GROUPS, D_STATE), cm.reshape(nb, SSM_GROUPS, D_STATE),
            xs, z, dskip, gssm)


def _post_kernel(x_ref, oa_ref, os_ref, sga_ref, sgb_ref, wa_ref, ws_ref, wo_ref, gpm_ref, gpre_ref,
                 wup_ref, wdn_ref, gpost_ref, y_ref, *, ff_chunk):
    merged = sga_ref[...] * _dot(oa_ref[...], wa_ref[...]) + sgb_ref[...] * _dot(os_ref[...], ws_ref[...])
    mix = _dot(merged.astype(BF16), wo_ref[...])
    x1 = x_ref[...] + _rms_scale(mix) * gpm_ref[...]
    h2 = (_rms_scale(x1) * gpre_ref[...]).astype(BF16)
    f = None
    for c in range(D_FF // ff_chunk):
        cs = slice(c * ff_chunk, (c + 1) * ff_chunk)
        u = jnp.square(jnp.maximum(_dot(h2, wup_ref[:, cs]), 0.0)).astype(BF16)
        d = _dot(u, wdn_ref[cs, :])
        f = d if f is None else f + d
    y_ref[...] = x1 + _rms_scale(f) * gpost_ref[...]


def _post(x, oa, os_, sga, sgb, wa, ws, wo, gpm, gpre, wup, wdn, gpost, *, tm, name, ff_chunk=1024):
    n = x.shape[0]
    row = lambda w: pl.BlockSpec((tm, w), lambda i: (i, 0))
    consts = (wa, ws, wo, gpm, gpre, wup, wdn, gpost)
    return pl.pallas_call(
        functools.partial(_post_kernel, ff_chunk=ff_chunk), grid=(n // tm,),
        in_specs=[row(D_MODEL), row(MLA_HEADS * V_DIM), row(D_INNER), row(D_MODEL), row(D_MODEL)]
        + [_const_spec(a.shape) for a in consts],
        out_specs=row(D_MODEL),
        out_shape=jax.ShapeDtypeStruct((n, D_MODEL), F32),
        compiler_params=_params(1), name=name)(x, oa, os_, sga, sgb, *consts)


def _rope_tables(pos):
    half = ROPE_DIM // 2
    inv_freq = ROPE_THETA ** (-jnp.arange(half, dtype=F32) / half)
    ang = pos.astype(F32)[:, None] * inv_freq[None, :]
    cos, sin = jnp.cos(ang), jnp.sin(ang)
    n = pos.shape[0]
    ones_lo = jnp.ones((n, NOPE_DIM), F32)
    pad_hi = jnp.ones((n, HEAD_PAD - NOPE_DIM - ROPE_DIM), F32)
    zeros = lambda w: jnp.zeros((n, w), F32)
    c = jnp.concatenate([ones_lo, cos, cos, pad_hi], axis=1)
    s1 = jnp.concatenate([zeros(NOPE_DIM), -sin, zeros(HEAD_PAD - NOPE_DIM - half)], axis=1)
    s2 = jnp.concatenate([zeros(NOPE_DIM + half), sin, zeros(HEAD_PAD - NOPE_DIM - ROPE_DIM)], axis=1)
    return c, s1, s2


def _pad_cols(w, lo, total):
    return jnp.pad(w, ((0, 0), (lo, total - lo - w.shape[1])))


def _row(v):
    return v.reshape(1, -1)


def kernel(x_prompt, x_sample, cache_kv_latent, cache_k_rope, state_ssm, state_conv, page_table, w_in, g_pre_mix, g_q, w_uq, g_kv, w_uk, w_uv, w_conv, b_conv, dt_bias, a_log, d_skip, g_ssm, w_br_attn, w_br_ssm, w_out, g_post_mix, g_pre_mlp, w_up, w_down, g_post_mlp):
    depth = w_in.shape[0]
    assert depth == 1
    bp, sp, _ = x_prompt.shape
    bs, ss, _ = x_sample.shape
    assert ss == 1
    past = page_table.shape[1] * PAGE_SIZE
    l = 0

    o_q, o_kv, o_pe = 0, Q_LORA, Q_LORA + KV_LORA
    o_z = o_pe + ROPE_DIM
    o_xbc = o_z + D_INNER
    o_dt = o_xbc + CONV_DIM
    o_ga = o_dt + SSM_HEADS
    o_gb = o_ga + D_MODEL
    wi = w_in[l]
    wlat = jnp.concatenate([wi[:, o_q:o_pe], _pad_cols(wi[:, o_pe:o_z], NOPE_DIM, HEAD_PAD)], axis=1).astype(BF16)
    wssm = (wi[:, o_z:o_xbc].astype(BF16), wi[:, o_xbc:o_dt].astype(BF16),
            _pad_cols(wi[:, o_dt:o_ga], 0, LANES).astype(BF16), wi[:, o_ga:o_gb].astype(BF16), wi[:, o_gb:].astype(BF16))
    wuq = jnp.pad(w_uq[l], ((0, 0), (0, 0), (0, HEAD_PAD - NOPE_DIM - ROPE_DIM))).reshape(Q_LORA, -1).astype(BF16)
    wuk = jnp.pad(w_uk[l], ((0, 0), (0, 0), (0, HEAD_PAD - NOPE_DIM))).reshape(KV_LORA, -1).astype(BF16)
    wuv = w_uv[l].reshape(KV_LORA, -1).astype(BF16)
    wv4 = w_uv[l].reshape(KV_LORA, MLA_HEADS // 2, 2, V_DIM)
    zv = jnp.zeros_like(wv4[:, :, 0])
    wuv_aug = jnp.stack([wv4[:, :, 0], zv, zv, wv4[:, :, 1]], axis=2).reshape(KV_LORA, -1).astype(BF16)
    v_ones = jnp.tile(jnp.concatenate([jnp.zeros((V_DIM,), F32), jnp.ones((2 * V_DIM,), F32),
                                       jnp.zeros((V_DIM,), F32)]), MLA_HEADS // 2).reshape(1, -1)
    gpre, gq, gkv = _row(g_pre_mix[l]), _row(g_q[l]), _row(g_kv[l])
    wconv, bconv = w_conv[l], _row(b_conv[l])
    dtb = _pad_cols(_row(dt_bias[l]), 0, LANES)
    alog = _pad_cols(_row(a_log[l]), 0, LANES)
    dskip = _row(jnp.repeat(d_skip[l], SSM_HEAD_DIM))
    gssm = _row(g_ssm[l])
    post_w = (w_br_attn[l].astype(BF16), w_br_ssm[l].astype(BF16), w_out[l].astype(BF16), _row(g_post_mix[l]),
              _row(g_pre_mlp[l]), w_up[l].astype(BF16), w_down[l].astype(BF16), _row(g_post_mlp[l]))

    xp = x_prompt.reshape(bp * sp, D_MODEL)
    tm_p = min(512, sp)
    tabs_p = _rope_tables(jnp.arange(sp, dtype=jnp.int32))
    qh_p, ckv_p, kpe_p, kh_p, v_p = _mla_proj(xp, gpre, wlat, gq, wuq, gkv, tabs_p, wuk, (wuv_aug, v_ones),
                                              prompt=True, tm=tm_p, tab_blocks=sp // tm_p)
    z_p, xsc_p, bc_p, dt_p, sga_p, sgb_p, conv_p = _ssm_proj(xp, gpre, wssm, wconv, bconv, tm=min(256, sp), seq=sp,
                                                             name="ssm_proj_prompt")
    oa_p = _attn_prompt(qh_p.reshape(bp, sp, -1), kh_p.reshape(bp, sp, -1), v_p.reshape(bp, sp, -1),
                        tq=min(256, sp))
    os_p, ssm_p = _ssd_prompt(xsc_p, bc_p, z_p, dt_p, dtb, alog, dskip, gssm, batch=bp, seq=sp)
    y_p = _post(xp, oa_p.reshape(bp * sp, -1), os_p, sga_p, sgb_p, *post_w, tm=min(256, sp), name="post_prompt")

    xs_in = x_sample.reshape(bs, D_MODEL)
    tabs_s = _rope_tables(jnp.full((bs,), past, dtype=jnp.int32))
    qh_s, ckv_s, kpe_s, qlat_s = _mla_proj(xs_in, gpre, wlat, gq, wuq, gkv, tabs_s, wuk, None,
                                           prompt=False, tm=bs, tab_blocks=1)
    z_s, xbc_s, dt_s, sga_s, sgb_s = _ssm_proj(xs_in, gpre, wssm, tm=bs, name="ssm_proj_sample")
    o_lat = _attn_sample(page_table, qlat_s, qh_s, ckv_s, kpe_s, cache_kv_latent[l],
                         jnp.swapaxes(cache_k_rope[l], 1, 2))
    oa_s = _uv_project(o_lat.reshape(bs, MLA_HEADS * KV_LORA), wuv)
    cnew_s, xs_s, xdt_t, bm_s, cm_s, dec_s = _ssd_sample_pre(
        xbc_s, jnp.swapaxes(state_conv[l], 0, 1), dt_s, wconv, bconv, dtb, alog)
    ssm_s, os_s = _ssd_sample_state(dec_s[:, :SSM_HEADS], state_ssm[l].reshape(bs, D_INNER, D_STATE),
                                    xdt_t, bm_s, cm_s, xs_s, z_s, dskip, gssm)
    y_s = _post(xs_in, oa_s, os_s, sga_s, sgb_s, *post_w, tm=bs, name="post_sample")

    hshape = (SSM_HEADS, SSM_HEAD_DIM, D_STATE)
    return (y_p.reshape(bp, sp, D_MODEL), y_s.reshape(bs, 1, D_MODEL),
            ckv_p.reshape(1, bp, sp, KV_LORA), jnp.swapaxes(kpe_p, 1, 2)[None],
            ssm_p.reshape(1, bp, *hshape), conv_p.reshape(1, bp, CONV_W - 1, CONV_DIM),
            ckv_s.reshape(1, bs, 1, KV_LORA), kpe_s.reshape(1, bs, 1, ROPE_DIM),
            ssm_s.reshape(1, bs, *hshape), jnp.swapaxes(cnew_s, 0, 1)[None])
```

```python
import functools
import math

import jax
import jax.numpy as jnp
from jax import lax
from jax.experimental import pallas as pl
from jax.experimental.pallas import tpu as pltpu

F32 = jnp.float32
BF16 = jnp.bfloat16

D_MODEL = 1024
MLA_HEADS = 16
Q_LORA = 384
KV_LORA = 256
NOPE_DIM = 64
ROPE_DIM = 32
V_DIM = 64
ROPE_THETA = 10000.0
SM_SCALE = 1.0 / math.sqrt(NOPE_DIM + ROPE_DIM)
_LOG2E = math.log2(math.e)
_EXP2_SCALE = SM_SCALE * _LOG2E
PAGE_SIZE = 128
D_INNER = 2048
SSM_HEAD_DIM = 64
SSM_HEADS = D_INNER // SSM_HEAD_DIM
SSM_GROUPS = 8
D_STATE = 128
CONV_W = 4
CONV_DIM = D_INNER + 2 * SSM_GROUPS * D_STATE
SSD_CHUNK = 128
D_FF = 4096
EPS = 1e-6

LANES = 128
SUBLANES = 8
HEAD_PAD = LANES
VMEM_LIMIT = 56 * 1024 * 1024

_NT = (((1,), (1,)), ((), ()))
_TN = (((0,), (0,)), ((), ()))


def _rms_scale(x):
    return x * lax.rsqrt(jnp.mean(x * x, axis=-1, keepdims=True) + EPS)


def _sigmoid(x):
    return 1.0 / (1.0 + jnp.exp2(x * (-_LOG2E)))


def _silu(x):
    return x * _sigmoid(x)


def _softplus(x):
    return jnp.maximum(x, 0.0) + jnp.log(1.0 + jnp.exp(-jnp.abs(x)))


def _dot(a, b):
    return jnp.dot(a, b, preferred_element_type=F32)


def _const_spec(shape):
    nd = len(shape)
    return pl.BlockSpec(shape, lambda *_: (0,) * nd, pipeline_mode=pl.Buffered(1))


def _params(n_grid, vmem=VMEM_LIMIT, flags=None):
    return pltpu.CompilerParams(dimension_semantics=("arbitrary",) * n_grid, vmem_limit_bytes=vmem, flags=flags)


def _rope_tile(t, c, s1, s2):
    return t * c + pltpu.roll(t, LANES - ROPE_DIM // 2, 1) * s1 + pltpu.roll(t, ROPE_DIM // 2, 1) * s2


def _mla_proj_kernel(*refs, prompt):
    (x_ref, gpre_ref, wlat_ref, gq_ref, wuq_ref, gkv_ref, rc_ref, rs1_ref, rs2_ref, wuk_ref) = refs[:10]
    if prompt:
        wuv_ref, vone_ref, qh_ref, ckv_ref, kpe_ref, kh_ref, v_ref = refs[10:]
    else:
        qh_ref, ckv_ref, kpe_ref, qlat_ref = refs[10:]
    h = (_rms_scale(x_ref[...]) * gpre_ref[...]).astype(BF16)
    lat = _dot(h, wlat_ref[...])
    cq = (_rms_scale(lat[:, :Q_LORA]) * gq_ref[...]).astype(BF16)
    ckv = _rms_scale(lat[:, Q_LORA:Q_LORA + KV_LORA]) * gkv_ref[...]
    ckv_ref[...] = ckv
    c, s1, s2 = rc_ref[...], rs1_ref[...], rs2_ref[...]
    kpe = _rope_tile(lat[:, Q_LORA + KV_LORA:], c, s1, s2)
    if prompt:
        kpe_ref[0] = kpe.T[NOPE_DIM:NOPE_DIM + ROPE_DIM, :]
    else:
        kpe_ref[...] = kpe[:, NOPE_DIM:NOPE_DIM + ROPE_DIM]
    ckv_b = ckv.astype(BF16)
    for hp in range(MLA_HEADS // 2):
        cols = slice(2 * hp * HEAD_PAD, (2 * hp + 2) * HEAD_PAD)
        q2 = _dot(cq, wuq_ref[:, cols])
        if prompt:
            k2 = _dot(ckv_b, wuk_ref[:, cols])
            v_ref[:, cols] = (_dot(ckv_b, wuv_ref[:, cols]) + vone_ref[:, cols]).astype(BF16)
        for j in range(2):
            sl = slice((2 * hp + j) * HEAD_PAD, (2 * hp + j + 1) * HEAD_PAD)
            loc = slice(j * HEAD_PAD, (j + 1) * HEAD_PAD)
            qh = _rope_tile(q2[:, loc], c, s1, s2)
            if prompt:
                qh = qh * _EXP2_SCALE
            qh = qh.astype(BF16)
            qh_ref[:, sl] = qh
            if prompt:
                kh_ref[:, sl] = (k2[:, loc] + kpe).astype(BF16)
            else:
                hd = 2 * hp + j
                qlat_ref[:, hd * KV_LORA:(hd + 1) * KV_LORA] = lax.dot_general(
                    qh, wuk_ref[:, sl], _NT, preferred_element_type=F32)


def _mla_proj(x, gpre, wlat, gq, wuq, gkv, rope_tabs, wuk, wuv, *, prompt, tm, tab_blocks):
    n = x.shape[0]
    hp = MLA_HEADS * HEAD_PAD
    row = lambda w: pl.BlockSpec((tm, w), lambda i: (i, 0))
    tab = pl.BlockSpec((tm, LANES), lambda i: (i % tab_blocks, 0))
    in_specs = [row(D_MODEL), _const_spec(gpre.shape), _const_spec(wlat.shape), _const_spec(gq.shape),
                _const_spec(wuq.shape), _const_spec(gkv.shape), tab, tab, tab, _const_spec(wuk.shape)]
    args = [x, gpre, wlat, gq, wuq, gkv, *rope_tabs, wuk]
    out_shape = [jax.ShapeDtypeStruct((n, hp), BF16), jax.ShapeDtypeStruct((n, KV_LORA), F32),
                 jax.ShapeDtypeStruct((n, ROPE_DIM), F32)]
    out_specs = [row(hp), row(KV_LORA), row(ROPE_DIM)]
    if prompt:
        out_shape[2] = jax.ShapeDtypeStruct((n // (tm * tab_blocks), ROPE_DIM, tm * tab_blocks), F32)
        out_specs[2] = pl.BlockSpec((1, ROPE_DIM, tm), lambda i: (i // tab_blocks, 0, i % tab_blocks))
        wuv_aug, v_ones = wuv
        in_specs += [_const_spec(wuv_aug.shape), _const_spec(v_ones.shape)]
        args += [wuv_aug, v_ones]
        out_shape += [jax.ShapeDtypeStruct((n, hp), BF16), jax.ShapeDtypeStruct((n, hp), BF16)]
        out_specs += [row(hp), row(hp)]
    else:
        out_shape.append(jax.ShapeDtypeStruct((n, MLA_HEADS * KV_LORA), F32))
        out_specs.append(row(MLA_HEADS * KV_LORA))
    return pl.pallas_call(
        functools.partial(_mla_proj_kernel, prompt=prompt),
        grid=(n // tm,), in_specs=in_specs, out_specs=out_specs, out_shape=out_shape,
        compiler_params=_params(1), name="mla_proj_prompt" if prompt else "mla_proj_sample")(*args)


_CONV_TAIL = CONV_W - 1
_CONV_COLS = 256


def _ssm_proj_kernel(*refs, conv, blocks_per_seq):
    x_ref, gpre_ref, wz_ref, wxbc_ref, wdt_ref, wga_ref, wgb_ref = refs[:7]
    if conv:
        (wconv_ref, bconv_ref, z_ref, xs_ref, bc_ref, dt_ref, sga_ref, sgb_ref, tail_ref, h_scr, xpad) = refs[7:]
    else:
        z_ref, xbc_ref, dt_ref, sga_ref, sgb_ref, h_scr = refs[7:]
    tm = x_ref.shape[0]
    cw = _CONV_COLS
    h_scr[...] = (_rms_scale(x_ref[...]) * gpre_ref[...]).astype(BF16)
    dt_ref[...] = _dot(h_scr[...], wdt_ref[...])
    if not conv:
        z_ref[...] = _dot(h_scr[...], wz_ref[...])
        xbc_ref[...] = _dot(h_scr[...], wxbc_ref[...])
        sga_ref[...] = _sigmoid(_dot(h_scr[...], wga_ref[...]))
        sgb_ref[...] = _sigmoid(_dot(h_scr[...], wgb_ref[...]))
        return

    first = pl.program_id(0) % blocks_per_seq == 0
    n_tiles = tm // SUBLANES
    n_xs = D_INNER // cw
    n_conv = CONV_DIM // cw
    assert 2 * (D_MODEL // cw) == n_conv - n_xs

    pad = _CONV_TAIL * SUBLANES

    @pl.when(first)
    def _():
        xpad[:, 0:pad, :] = jnp.zeros((n_conv, pad, cw), F32)

    @pl.when(jnp.logical_not(first))
    def _():
        xpad[:, pad - SUBLANES:pad, :] = xpad[:, tm + pad - SUBLANES:tm + pad, :]

    def cols_of(i):
        return slice(i * cw, (i + 1) * cw)

    def project_chunk(i):
        xpad[i, pad:, :] = _dot(h_scr[...], wxbc_ref[:, cols_of(i)])

    assert CONV_W == 4

    def shift_rows(v, s):
        r = pltpu.roll(v, s, 1)
        sub = lax.broadcasted_iota(jnp.int32, (v.shape[0] - 1, SUBLANES, cw), 1)
        return jnp.where(sub >= s, r[1:], r[:-1])

    def conv_chunk(i, out_ref, out_i):
        cols = cols_of(i)
        tiles = xpad[i].reshape(pad // SUBLANES + n_tiles, SUBLANES, cw)[1:]
        w = [wconv_ref[k:k + 1, cols] for k in range(CONV_W)]
        x1 = shift_rows(tiles, 1)
        a = tiles[2:] * w[3] + x1[1:] * w[2]
        b = tiles[1:] * w[1] + x1 * w[0]
        acc = (a + shift_rows(b, 2)).reshape(tm, cw) + bconv_ref[:, cols]
        out_ref[:, cols_of(out_i)] = _silu(acc).astype(out_ref.dtype)
        tail_ref[0, :, cols] = xpad[i, pad + tm - _CONV_TAIL:pad + tm, :]

    def stage(i, conv_out, conv_base, w_ref, out_ref, out_base, act):
        if i + 1 < n_conv:
            project_chunk(i + 1)
        oc = cols_of(i - out_base)
        out_ref[:, oc] = act(_dot(h_scr[...], w_ref[:, oc]))
        conv_chunk(i, conv_out, i - conv_base)

    ident = lambda v: v
    n_g = D_MODEL // cw
    project_chunk(0)
    for i in range(n_conv):
        if i < n_xs:
            stage(i, xs_ref, 0, wz_ref, z_ref, 0, ident)
        elif i < n_xs + n_g:
            stage(i, bc_ref, n_xs, wga_ref, sga_ref, n_xs, _sigmoid)
        else:
            stage(i, bc_ref, n_xs, wgb_ref, sgb_ref, n_xs + n_g, _sigmoid)


def _ssm_proj(x, gpre, wssm, wconv=None, bconv=None, *, tm, seq=None, name):
    n = x.shape[0]
    conv = wconv is not None
    row = lambda w, dt=F32: (pl.BlockSpec((tm, w), lambda i: (i, 0)), jax.ShapeDtypeStruct((n, w), dt))
    in_specs = ([pl.BlockSpec((tm, D_MODEL), lambda i: (i, 0)), _const_spec(gpre.shape)]
                + [_const_spec(w.shape) for w in wssm])
    args = [x, gpre, *wssm]
    scratch = [pltpu.VMEM((tm, D_MODEL), BF16)]
    if conv:
        bps = seq // tm
        in_specs += [_const_spec(wconv.shape), _const_spec(bconv.shape)]
        args += [wconv, bconv]
        outs = [row(D_INNER), row(D_INNER), row(CONV_DIM - D_INNER, BF16), row(LANES), row(D_MODEL), row(D_MODEL),
                (pl.BlockSpec((1, _CONV_TAIL, CONV_DIM), lambda i: (i // bps, 0, 0)),
                 jax.ShapeDtypeStruct((n // seq, _CONV_TAIL, CONV_DIM), F32))]
        scratch += [pltpu.VMEM((CONV_DIM // _CONV_COLS, _CONV_TAIL * SUBLANES + tm, _CONV_COLS), F32)]
    else:
        bps = 1
        outs = [row(D_INNER), row(CONV_DIM), row(LANES), row(D_MODEL), row(D_MODEL)]
    return pl.pallas_call(
        functools.partial(_ssm_proj_kernel, conv=conv, blocks_per_seq=bps), grid=(n // tm,),
        in_specs=in_specs, out_specs=[o[0] for o in outs], out_shape=[o[1] for o in outs],
        scratch_shapes=scratch, compiler_params=_params(1), name=name)(*args)


def _fold_lanes(x, op):
    out = x[:, :LANES]
    for c in range(1, x.shape[1] // LANES):
        out = op(out, x[:, c * LANES:(c + 1) * LANES])
    return out


def _attn_prompt_kernel(q_ref, k_ref, v_ref, o_ref, s_scr, p_scr, o_scr, *, seq, tq):
    lane = lax.broadcasted_iota(jnp.int32, (tq, LANES), 1)
    row = lax.broadcasted_iota(jnp.int32, (tq, tq), 0)
    col = lax.broadcasted_iota(jnp.int32, (tq, tq), 1)
    causal = col <= row
    chains = [(qi, j) for qi in range(seq // tq) for j in range(2)]
    row_max = {}

    def scores(c):
        qi, j = chains[c]
        hs = slice(j * HEAD_PAD, (j + 1) * HEAD_PAD)
        q = q_ref[0, qi * tq:(qi + 1) * tq, hs]
        mx = None
        for kt in range(qi + 1):
            ks = slice(kt * tq, (kt + 1) * tq)
            s = lax.dot_general(q, k_ref[0, ks, hs], _NT, preferred_element_type=F32)
            if kt == qi:
                s = jnp.where(causal, s, -jnp.inf)
            s_scr[c % 2, :, ks] = s
            m2 = _fold_lanes(s, jnp.maximum)
            mx = m2 if mx is None else jnp.maximum(mx, m2)
        row_max[c] = jnp.max(mx, axis=-1, keepdims=True)

    def numerators(c):
        qi, _ = chains[c]
        m = row_max.pop(c)
        for kt in range(qi + 1):
            ks = slice(kt * tq, (kt + 1) * tq)
            p_scr[c % 2, :, ks] = jnp.exp2(s_scr[c % 2, :, ks] - m).astype(BF16)

    def weighted_values(c):
        qi, j = chains[c]
        lo, hi = qi * tq, (qi + 1) * tq
        acc = _dot(p_scr[c % 2, :, :hi], v_ref[0, :hi, j * LANES:(j + 1) * LANES])
        out = acc / pltpu.roll(acc, V_DIM, 1)
        if j == 0:
            o_scr[...] = out
        else:
            o_ref[0, lo:hi, :] = jnp.where(lane < V_DIM, o_scr[...], out).astype(BF16)

    n = len(chains)
    for step in range(n + 2):
        if step < n:
            scores(step)
        if 1 <= step <= n:
            numerators(step - 1)
        if step >= 2:
            weighted_values(step - 2)


def _attn_prompt(qh, kh, v, *, tq=256):
    b, s, _ = qh.shape
    pair = 2 * HEAD_PAD
    return pl.pallas_call(
        functools.partial(_attn_prompt_kernel, seq=s, tq=tq),
        grid=(b, MLA_HEADS // 2),
        in_specs=[pl.BlockSpec((1, s, pair), lambda i, j: (i, 0, j)),
                  pl.BlockSpec((1, s, pair), lambda i, j: (i, 0, j)),
                  pl.BlockSpec((1, s, pair), lambda i, j: (i, 0, j))],
        out_specs=pl.BlockSpec((1, s, 2 * V_DIM), lambda i, j: (i, 0, j)),
        out_shape=jax.ShapeDtypeStruct((b, s, MLA_HEADS * V_DIM), BF16),
        scratch_shapes=[pltpu.VMEM((2, tq, s), F32), pltpu.VMEM((2, tq, s), BF16),
                        pltpu.VMEM((tq, LANES), F32)],
        compiler_params=_params(2), name="attn_prompt")(qh, kh, v)


def _attn_sample_kernel(pt_ref, qlat_ref, qh_ref, ckv_ref, kpe_ref, ckv_prev_ref, ckv_hbm, cpet_hbm, o_ref,
                        kv0, kv1, pe0, pe1, kvb0, kvb1, s0, s1, st0, st1, sem, *, n_pages, tile, nb):
    g = pl.program_id(0)
    past = n_pages * PAGE_SIZE
    n_tiles = past // tile
    kv, pe, kvb, sc_scr, st = (kv0, kv1), (pe0, pe1), (kvb0, kvb1), (s0, s1), (st0, st1)

    def start_fetch(seq, r):
        for j in range(n_pages):
            pg = pt_ref[seq, j]
            rows = pl.ds(j * PAGE_SIZE, PAGE_SIZE)
            pltpu.make_async_copy(ckv_hbm.at[pg], kv[r].at[rows], sem.at[0, r]).start()
            pltpu.make_async_copy(cpet_hbm.at[pg], pe[r].at[:, rows], sem.at[1, r]).start()

    def wait_fetch(r):
        pltpu.make_async_copy(kv[r], kv[r], sem.at[0, r]).wait()
        pltpu.make_async_copy(pe[r], pe[r], sem.at[1, r]).wait()

    @pl.when(g == 0)
    def _():
        start_fetch(0, 0)
        kvb1[...] = jnp.zeros(kvb1.shape, BF16)
        s1[...] = jnp.zeros(s1.shape, F32)
        st1[...] = jnp.zeros(st1.shape, F32)

    def half_step(r):
        h = 2 * g + r
        p_ = 1 - r
        start_fetch(jnp.minimum(h + 1, nb - 1), p_)
        wait_fetch(r)

        qlat = qlat_ref[r]
        qlat_b = qlat.astype(BF16)
        qpe_b = qh_ref[r][:, NOPE_DIM:NOPE_DIM + ROPE_DIM]
        m_prev = st[p_][0][:, :1]
        snew_prev = st[p_][1][:, :1]
        ckv_prev = ckv_prev_ref[r]
        p_new = jnp.exp2((snew_prev - m_prev) * _EXP2_SCALE)
        psum = None
        accs = [p_new * ckv_prev, None]
        mx = None
        for t in range(n_tiles):
            ts = slice(t * tile, (t + 1) * tile)
            kc = kv[r][ts, :].astype(BF16)
            kvb[r][ts, :] = kc
            sc = (lax.dot_general(qlat_b, kc, _NT, preferred_element_type=F32)
                  + _dot(qpe_b, pe[r][:, ts].astype(BF16)))
            sc_scr[r][:, ts] = sc
            mx = sc if mx is None else jnp.maximum(mx, sc)
            p = jnp.exp2((sc_scr[p_][:, ts] - m_prev) * _EXP2_SCALE)
            psum = p if psum is None else psum + p
            d = _dot(p.astype(BF16), kvb[p_][ts, :])
            accs[t % 2] = d if accs[t % 2] is None else accs[t % 2] + d
        ckv_new = ckv_ref[r]
        kpe_new = kpe_ref[r]
        s_new = (jnp.sum(qlat * ckv_new, axis=-1, keepdims=True)
                 + jnp.sum(qpe_b.astype(F32) * kpe_new, axis=-1, keepdims=True))
        m = jnp.maximum(jnp.max(mx, axis=-1, keepdims=True), s_new)
        st[r][0] = jnp.broadcast_to(m, (MLA_HEADS, LANES))
        st[r][1] = jnp.broadcast_to(s_new, (MLA_HEADS, LANES))
        den = p_new + jnp.sum(psum, axis=-1, keepdims=True)
        acc = accs[0] if accs[1] is None else accs[0] + accs[1]
        o_ref[r] = acc / den

    half_step(0)
    half_step(1)

    @pl.when(g == pl.num_programs(0) - 1)
    def _():
        wait_fetch(0)


def _attn_sample(page_table, qlat, qh, ckv, kpe, cache_kv, cache_pe_t, *, tile=256):
    nb, n_pages = page_table.shape
    assert nb % 2 == 0
    past = n_pages * PAGE_SIZE
    last = nb // 2 - 1
    cur = lambda i, pt: (jnp.minimum(i, last), 0, 0)
    pair = lambda w: pl.BlockSpec((2, MLA_HEADS, w), cur)
    vec = lambda w: pl.BlockSpec((2, 1, w), cur)
    slot = lambda shape, dt: [pltpu.VMEM(shape, dt)] * 2
    grid_spec = pltpu.PrefetchScalarGridSpec(
        num_scalar_prefetch=1, grid=(nb // 2 + 1,),
        in_specs=[pair(KV_LORA), pair(HEAD_PAD), vec(KV_LORA), vec(ROPE_DIM),
                  pl.BlockSpec((2, 1, KV_LORA), lambda i, pt: (i, 0, 0)),
                  pl.BlockSpec(memory_space=pl.ANY),
                  pl.BlockSpec(memory_space=pl.ANY)],
        out_specs=pl.BlockSpec((2, MLA_HEADS, KV_LORA), lambda i, pt: (i, 0, 0)),
        scratch_shapes=(slot((past, KV_LORA), F32) + slot((ROPE_DIM, past), F32) + slot((past, KV_LORA), BF16)
                        + slot((MLA_HEADS, past), F32) + slot((2, MLA_HEADS, LANES), F32)
                        + [pltpu.SemaphoreType.DMA((2, 2))]))
    ckv3 = ckv.reshape(nb, 1, KV_LORA)
    out = pl.pallas_call(
        functools.partial(_attn_sample_kernel, n_pages=n_pages, tile=min(tile, past), nb=nb),
        grid_spec=grid_spec,
        out_shape=jax.ShapeDtypeStruct((nb + 2, MLA_HEADS, KV_LORA), F32),
        compiler_params=_params(1), name="attn_sample")(
            page_table, qlat.reshape(nb, MLA_HEADS, KV_LORA), qh.reshape(nb, MLA_HEADS, HEAD_PAD),
            ckv3, kpe.reshape(nb, 1, ROPE_DIM), jnp.pad(ckv3, ((1, 1), (0, 0), (0, 0))), cache_kv, cache_pe_t)
    return out[1:nb + 1]


def _uv_kernel(ol_ref, wuv_ref, o_ref):
    lane = lax.broadcasted_iota(jnp.int32, o_ref.shape[:1] + (LANES,), 1)
    for hp in range(MLA_HEADS // 2):
        w = wuv_ref[:, hp * LANES:(hp + 1) * LANES]
        a0 = _dot(ol_ref[:, (2 * hp) * KV_LORA:(2 * hp + 1) * KV_LORA].astype(BF16), w)
        a1 = _dot(ol_ref[:, (2 * hp + 1) * KV_LORA:(2 * hp + 2) * KV_LORA].astype(BF16), w)
        o_ref[:, hp * LANES:(hp + 1) * LANES] = jnp.where(lane < V_DIM, a0, a1).astype(BF16)


def _uv_project(o_lat, wuv):
    n = o_lat.shape[0]
    return pl.pallas_call(
        _uv_kernel, grid=(1,),
        in_specs=[_const_spec(o_lat.shape), _const_spec(wuv.shape)],
        out_specs=pl.BlockSpec((n, MLA_HEADS * V_DIM), lambda i: (0, 0)),
        out_shape=jax.ShapeDtypeStruct((n, MLA_HEADS * V_DIM), BF16),
        compiler_params=_params(1), name="uv_project")(o_lat, wuv)


def _cumsum_rows(v):
    rowi = lax.broadcasted_iota(jnp.int32, v.shape, 0)
    k = 1
    while k < v.shape[0]:
        v = v + jnp.where(rowi >= k, pltpu.roll(v, k, 0), 0.0)
        k *= 2
    return v


def _pair_cols(mat, h0, lane):
    t = mat.shape[0]
    a = jnp.broadcast_to(mat[:, h0:h0 + 1], (t, LANES))
    b = jnp.broadcast_to(mat[:, h0 + 1:h0 + 2], (t, LANES))
    return jnp.where(lane < SSM_HEAD_DIM, a, b)


def _expand_heads(x, e_ref, lane):
    hi = x.astype(BF16).astype(F32)
    r1 = x - hi
    mid = r1.astype(BF16).astype(F32)
    lo = r1 - mid
    packed = jnp.where(lane < SSM_HEADS, hi,
                       jnp.where(lane < 2 * SSM_HEADS, pltpu.roll(mid, SSM_HEADS, 1),
                                 jnp.where(lane < 3 * SSM_HEADS, pltpu.roll(lo, 2 * SSM_HEADS, 1), 0.0)))
    return _dot(packed.astype(BF16), e_ref[...])


def _head_expander(width):
    r = jnp.arange(LANES)[:, None]
    c = jnp.arange(SSM_HEADS * width)[None, :]
    return ((r < 3 * SSM_HEADS) & (r % SSM_HEADS == c // width)).astype(BF16)


def _ssd_prompt_kernel(xs_ref, bc_ref, z_ref, dt_ref, dtb_ref, alog_ref, dskip_ref, gssm_ref, e128_ref, e64_ref,
                       y_ref, state_ref, state):
    @pl.when(pl.program_id(1) == 0)
    def _():
        state[...] = jnp.zeros(state.shape, F32)

    for sub in range(xs_ref.shape[0] // SSD_CHUNK):
        _ssd_chunk(slice(sub * SSD_CHUNK, (sub + 1) * SSD_CHUNK), xs_ref, bc_ref, z_ref, dt_ref, dtb_ref, alog_ref,
                   dskip_ref, gssm_ref, e128_ref, e64_ref, y_ref, state)
    state_ref[0] = state[...]


def _ssd_chunk(rows, xs_ref, bc_ref, z_ref, dt_ref, dtb_ref, alog_ref, dskip_ref, gssm_ref, e128_ref, e64_ref,
               y_ref, state):
    t = SSD_CHUNK
    dt = _softplus(dt_ref[rows, :] + dtb_ref[...])
    a = -jnp.exp(alog_ref[...])
    cs = _cumsum_rows(dt * a) * _LOG2E
    cs_last = cs[t - 1:t, :]
    ecs = jnp.exp2(cs)
    wloc = dt * jnp.exp2(cs_last - cs)
    cs_t = cs.T
    dt_t = dt.T
    cdec = jnp.broadcast_to(jnp.exp2(cs_t[:, t - 1:t]), (LANES, LANES))

    lane = lax.broadcasted_iota(jnp.int32, (t, LANES), 1)
    tri = lax.broadcasted_iota(jnp.int32, (t, t), 0) >= lax.broadcasted_iota(jnp.int32, (t, t), 1)
    hpg = SSM_HEADS // SSM_GROUPS
    gw = hpg * SSM_HEAD_DIM
    c_off = SSM_GROUPS * D_STATE
    cs_x = _expand_heads(cs, e128_ref, lane)
    ecs_x = _expand_heads(ecs, e64_ref, lane)
    wloc_x = _expand_heads(wloc, e64_ref, lane)

    for g in range(SSM_GROUPS):
        bm_b = bc_ref[rows, g * D_STATE:(g + 1) * D_STATE]
        cm_b = bc_ref[rows, c_off + g * D_STATE:c_off + (g + 1) * D_STATE]
        cb = lax.dot_general(cm_b, bm_b, _NT, preferred_element_type=F32)
        prev_b = state[g * gw:(g + 1) * gw, :].astype(BF16)
        yoff = lax.dot_general(cm_b, prev_b, _NT, preferred_element_type=F32)
        ys = []
        for pr in range(hpg // 2):
            h0 = g * hpg + 2 * pr
            x2 = xs_ref[rows, h0 * SSM_HEAD_DIM:(h0 + 2) * SSM_HEAD_DIM]
            x2_b = x2.astype(BF16)
            yd = []
            for h in (h0, h0 + 1):
                seg = cs_x[:, h * LANES:(h + 1) * LANES] - cs_t[h:h + 1, :]
                dec = jnp.exp2(jnp.where(tri, seg, -jnp.inf))
                mh = (cb * dec * dt_t[h:h + 1, :]).astype(BF16)
                yd.append(_dot(mh, x2_b))
            pair = slice(h0 * SSM_HEAD_DIM, (h0 + 2) * SSM_HEAD_DIM)
            y2 = jnp.where(lane < SSM_HEAD_DIM, yd[0], yd[1])
            y2 = y2 + yoff[:, 2 * pr * SSM_HEAD_DIM:(2 * pr + 2) * SSM_HEAD_DIM] * ecs_x[:, pair]
            y2 = y2 + dskip_ref[:, pair] * x2
            ys.append(y2)
            xw = (x2 * wloc_x[:, pair]).astype(BF16)
            loc = lax.dot_general(xw, bm_b, _TN, preferred_element_type=F32)
            r0 = h0 * SSM_HEAD_DIM
            for k, h in enumerate((h0, h0 + 1)):
                rs = slice(r0 + k * SSM_HEAD_DIM, r0 + (k + 1) * SSM_HEAD_DIM)
                state[rs, :] = (state[rs, :] * cdec[h:h + 1, :]
                                + loc[k * SSM_HEAD_DIM:(k + 1) * SSM_HEAD_DIM, :])
        yg = jnp.concatenate(ys, axis=1) * _silu(z_ref[rows, g * gw:(g + 1) * gw])
        y_ref[rows, g * gw:(g + 1) * gw] = (_rms_scale(yg) * gssm_ref[:, g * gw:(g + 1) * gw]).astype(BF16)


def _ssd_prompt(xs, bc, z, dt, dtb, alog, dskip, gssm, *, batch, seq, chunks_per_step=2):
    rows = SSD_CHUNK * math.gcd(chunks_per_step, seq // SSD_CHUNK)
    nc = seq // rows
    row = lambda w: pl.BlockSpec((rows, w), lambda b, c: (b * nc + c, 0))
    e128, e64 = _head_expander(LANES), _head_expander(SSM_HEAD_DIM)
    return pl.pallas_call(
        _ssd_prompt_kernel, grid=(batch, nc),
        in_specs=[row(D_INNER), row(CONV_DIM - D_INNER), row(D_INNER), row(LANES)]
        + [_const_spec(a.shape) for a in (dtb, alog, dskip, gssm, e128, e64)],
        out_specs=[row(D_INNER), pl.BlockSpec((1, D_INNER, D_STATE), lambda b, c: (b, 0, 0))],
        out_shape=[jax.ShapeDtypeStruct((batch * seq, D_INNER), BF16),
                   jax.ShapeDtypeStruct((batch, D_INNER, D_STATE), F32)],
        scratch_shapes=[pltpu.VMEM((D_INNER, D_STATE), F32)],
        compiler_params=_params(2), name="ssd_prompt")(xs, bc, z, dt, dtb, alog, dskip, gssm, e128, e64)


def _ssd_sample_pre_kernel(xbc_ref, cprev_ref, dt_ref, wconv_ref, bconv_ref, dtb_ref, alog_ref,
                           cnew_ref, xs_ref, xdt_t_ref, bm_ref, cm_ref, dec_ref):
    tail = CONV_W - 1
    acc = bconv_ref[...]
    for k in range(tail):
        acc = acc + cprev_ref[k] * wconv_ref[k:k + 1, :]
    x_new = xbc_ref[...]
    acc = acc + x_new * wconv_ref[tail:tail + 1, :]
    u = _silu(acc)
    for k in range(1, tail):
        cnew_ref[k - 1] = cprev_ref[k]
    cnew_ref[tail - 1] = x_new
    xs = u[:, :D_INNER]
    xs_ref[...] = xs
    bm_ref[...] = u[:, D_INNER:D_INNER + SSM_GROUPS * D_STATE]
    cm_ref[...] = u[:, D_INNER + SSM_GROUPS * D_STATE:]
    dt = _softplus(dt_ref[...] + dtb_ref[...])
    dec_ref[...] = jnp.exp(dt * (-jnp.exp(alog_ref[...])))
    nb = xs.shape[0]
    lane = lax.broadcasted_iota(jnp.int32, (nb, LANES), 1)
    for pr in range(SSM_HEADS // 2):
        x2 = xs[:, pr * LANES:(pr + 1) * LANES] * _pair_cols(dt, 2 * pr, lane)
        xdt_t_ref[pr * LANES:(pr + 1) * LANES, :] = x2.T


def _ssd_sample_pre(xbc, cprev, dt, wconv, bconv, dtb, alog):
    nb = xbc.shape[0]
    ins = (xbc, cprev, dt, wconv, bconv, dtb, alog)
    shapes = [((CONV_W - 1, nb, CONV_DIM), F32), ((nb, D_INNER), F32), ((D_INNER, nb), F32),
              ((nb, SSM_GROUPS * D_STATE), F32), ((nb, SSM_GROUPS * D_STATE), F32), ((nb, LANES), F32)]
    return pl.pallas_call(
        _ssd_sample_pre_kernel, grid=(1,),
        in_specs=[_const_spec(a.shape) for a in ins],
        out_specs=[pl.BlockSpec(s, lambda i, nd=len(s): (0,) * nd) for s, _ in shapes],
        out_shape=[jax.ShapeDtypeStruct(s, d) for s, d in shapes],
        compiler_params=_params(1), name="ssd_sample_pre")(*ins)


def _ssd_sample_state_kernel(dec_ref, st_ref, xdt_t_ref, bm_ref, cm_ref, xs_ref, z_ref, dskip_ref, gssm_ref,
                             st_out_ref, y_ref, y_t):
    per_step = st_ref.shape[0]
    nb = xs_ref.shape[0]
    hpg = SSM_HEADS // SSM_GROUPS
    gw = hpg * SSM_HEAD_DIM
    lane = lax.broadcasted_iota(jnp.int32, (gw, nb), 1)
    groups = [slice(g * gw, (g + 1) * gw) for g in range(SSM_GROUPS)]
    for i in range(per_step):
        b = pl.program_id(0) * per_step + i
        is_b = lane == b
        x_cols = [jnp.sum(jnp.where(is_b, xdt_t_ref[rows, :], 0.0), axis=1, keepdims=True) for rows in groups]
        y_cols = []
        for g, rows in enumerate(groups):
            pieces = []
            for hh in range(hpg):
                h = g * hpg + hh
                rs = slice(h * SSM_HEAD_DIM, (h + 1) * SSM_HEAD_DIM)
                pieces.append(st_ref[i, rs, :] * dec_ref[b, h])
            new = jnp.concatenate(pieces, axis=0) + x_cols[g] * bm_ref[i, g:g + 1, :]
            st_out_ref[i, rows, :] = new
            y_cols.append(jnp.sum(new * cm_ref[i, g:g + 1, :], axis=1, keepdims=True))
        for rows, y_col in zip(groups, y_cols):
            y_t[rows, :] = jnp.where(is_b, y_col, y_t[rows, :])

    @pl.when(pl.program_id(0) == pl.num_programs(0) - 1)
    def _():
        for g in range(SSM_GROUPS):
            cols = slice(g * gw, (g + 1) * gw)
            y = y_t[cols, :].T + dskip_ref[:, cols] * xs_ref[:, cols]
            y = y * _silu(z_ref[:, cols])
            y_ref[:, cols] = (_rms_scale(y) * gssm_ref[:, cols]).astype(BF16)


def _ssd_sample_state(dec, state, xdt_t, bm, cm, xs, z, dskip, gssm, *, per_step=4):
    nb = state.shape[0]
    per_step = math.gcd(per_step, nb)
    grid_spec = pl.GridSpec(
        grid=(nb // per_step,),
        in_specs=[pl.BlockSpec(memory_space=pltpu.SMEM),
                  pl.BlockSpec((per_step, D_INNER, D_STATE), lambda i: (i, 0, 0)),
                  _const_spec(xdt_t.shape),
                  pl.BlockSpec((per_step, SSM_GROUPS, D_STATE), lambda i: (i, 0, 0)),
                  pl.BlockSpec((per_step, SSM_GROUPS, D_STATE), lambda i: (i, 0, 0)),
                  _const_spec(xs.shape), _const_spec(z.shape), _const_spec(dskip.shape), _const_spec(gssm.shape)],
        out_specs=[pl.BlockSpec((per_step, D_INNER, D_STATE), lambda i: (i, 0, 0)),
                   pl.BlockSpec((nb, D_INNER), lambda i: (0, 0))],
        scratch_shapes=[pltpu.VMEM((D_INNER, nb), F32)])
    return pl.pallas_call(
        _ssd_sample_state_kernel, grid_spec=grid_spec,
        out_shape=[jax.ShapeDtypeStruct(state.shape, F32), jax.ShapeDtypeStruct((nb, D_INNER), BF16)],
        compiler_params=_params(1), name="ssd_sample_state")(
            dec, state, xdt_t, bm.reshape(nb, SSM_GROUPS, D_STATE), cm.reshape(nb, SSM_GROUPS, D_STATE),
            xs, z, dskip, gssm)


def _post_kernel(x_ref, oa_ref, os_ref, sga_ref, sgb_ref, wa_ref, ws_ref, wo_ref, gpm_ref, gpre_ref,
                 wup_ref, wdn_ref, gpost_ref, y_ref, *, ff_chunk):
    merged = sga_ref[...] * _dot(oa_ref[...], wa_ref[...]) + sgb_ref[...] * _dot(os_ref[...], ws_ref[...])
    mix = _dot(merged.astype(BF16), wo_ref[...])
    x1 = x_ref[...] + _rms_scale(mix) * gpm_ref[...]
    h2 = (_rms_scale(x1) * gpre_ref[...]).astype(BF16)
    f = None
    for c in range(D_FF // ff_chunk):
        cs = slice(c * ff_chunk, (c + 1) * ff_chunk)
        u = jnp.square(jnp.maximum(_dot(h2, wup_ref[:, cs]), 0.0)).astype(BF16)
        d = _dot(u, wdn_ref[cs, :])
        f = d if f is None else f + d
    y_ref[...] = x1 + _rms_scale(f) * gpost_ref[...]


def _post(x, oa, os_, sga, sgb, wa, ws, wo, gpm, gpre, wup, wdn, gpost, *, tm, name, ff_chunk=1024):
    n = x.shape[0]
    row = lambda w: pl.BlockSpec((tm, w), lambda i: (i, 0))
    consts = (wa, ws, wo, gpm, gpre, wup, wdn, gpost)
    return pl.pallas_call(
        functools.partial(_post_kernel, ff_chunk=ff_chunk), grid=(n // tm,),
        in_specs=[row(D_MODEL), row(MLA_HEADS * V_DIM), row(D_INNER), row(D_MODEL), row(D_MODEL)]
        + [_const_spec(a.shape) for a in consts],
        out_specs=row(D_MODEL),
        out_shape=jax.ShapeDtypeStruct((n, D_MODEL), F32),
        compiler_params=_params(1), name=name)(x, oa, os_, sga, sgb, *consts)


def _rope_tables(pos):
    half = ROPE_DIM // 2
    inv_freq = ROPE_THETA ** (-jnp.arange(half, dtype=F32) / half)
    ang = pos.astype(F32)[:, None] * inv_freq[None, :]
    cos, sin = jnp.cos(ang), jnp.sin(ang)
    n = pos.shape[0]
    ones_lo = jnp.ones((n, NOPE_DIM), F32)
    pad_hi = jnp.ones((n, HEAD_PAD - NOPE_DIM - ROPE_DIM), F32)
    zeros = lambda w: jnp.zeros((n, w), F32)
    c = jnp.concatenate([ones_lo, cos, cos, pad_hi], axis=1)
    s1 = jnp.concatenate([zeros(NOPE_DIM), -sin, zeros(HEAD_PAD - NOPE_DIM - half)], axis=1)
    s2 = jnp.concatenate([zeros(NOPE_DIM + half), sin, zeros(HEAD_PAD - NOPE_DIM - ROPE_DIM)], axis=1)
    return c, s1, s2


def _pad_cols(w, lo, total):
    return jnp.pad(w, ((0, 0), (lo, total - lo - w.shape[1])))


def _row(v):
    return v.reshape(1, -1)


def kernel(x_prompt, x_sample, cache_kv_latent, cache_k_rope, state_ssm, state_conv, page_table, w_in, g_pre_mix, g_q, w_uq, g_kv, w_uk, w_uv, w_conv, b_conv, dt_bias, a_log, d_skip, g_ssm, w_br_attn, w_br_ssm, w_out, g_post_mix, g_pre_mlp, w_up, w_down, g_post_mlp):
    depth = w_in.shape[0]
    assert depth == 1
    bp, sp, _ = x_prompt.shape
    bs, ss, _ = x_sample.shape
    assert ss == 1
    past = page_table.shape[1] * PAGE_SIZE
    l = 0

    o_q, o_kv, o_pe = 0, Q_LORA, Q_LORA + KV_LORA
    o_z = o_pe + ROPE_DIM
    o_xbc = o_z + D_INNER
    o_dt = o_xbc + CONV_DIM
    o_ga = o_dt + SSM_HEADS
    o_gb = o_ga + D_MODEL
    wi = w_in[l]
    wlat = jnp.concatenate([wi[:, o_q:o_pe], _pad_cols(wi[:, o_pe:o_z], NOPE_DIM, HEAD_PAD)], axis=1).astype(BF16)
    wssm = (wi[:, o_z:o_xbc].astype(BF16), wi[:, o_xbc:o_dt].astype(BF16),
            _pad_cols(wi[:, o_dt:o_ga], 0, LANES).astype(BF16), wi[:, o_ga:o_gb].astype(BF16), wi[:, o_gb:].astype(BF16))
    wuq = jnp.pad(w_uq[l], ((0, 0), (0, 0), (0, HEAD_PAD - NOPE_DIM - ROPE_DIM))).reshape(Q_LORA, -1).astype(BF16)
    wuk = jnp.pad(w_uk[l], ((0, 0), (0, 0), (0, HEAD_PAD - NOPE_DIM))).reshape(KV_LORA, -1).astype(BF16)
    wuv = w_uv[l].reshape(KV_LORA, -1).astype(BF16)
    wv4 = w_uv[l].reshape(KV_LORA, MLA_HEADS // 2, 2, V_DIM)
    zv = jnp.zeros_like(wv4[:, :, 0])
    wuv_aug = jnp.stack([wv4[:, :, 0], zv, zv, wv4[:, :, 1]], axis=2).reshape(KV_LORA, -1).astype(BF16)
    v_ones = jnp.tile(jnp.concatenate([jnp.zeros((V_DIM,), F32), jnp.ones((2 * V_DIM,), F32),
                                       jnp.zeros((V_DIM,), F32)]), MLA_HEADS // 2).reshape(1, -1)
    gpre, gq, gkv = _row(g_pre_mix[l]), _row(g_q[l]), _row(g_kv[l])
    wconv, bconv = w_conv[l], _row(b_conv[l])
    dtb = _pad_cols(_row(dt_bias[l]), 0, LANES)
    alog = _pad_cols(_row(a_log[l]), 0, LANES)
    dskip = _row(jnp.repeat(d_skip[l], SSM_HEAD_DIM))
    gssm = _row(g_ssm[l])
    post_w = (w_br_attn[l].astype(BF16), w_br_ssm[l].astype(BF16), w_out[l].astype(BF16), _row(g_post_mix[l]),
              _row(g_pre_mlp[l]), w_up[l].astype(BF16), w_down[l].astype(BF16), _row(g_post_mlp[l]))

    xp = x_prompt.reshape(bp * sp, D_MODEL)
    tm_p = min(512, sp)
    tabs_p = _rope_tables(jnp.arange(sp, dtype=jnp.int32))
    qh_p, ckv_p, kpe_p, kh_p, v_p = _mla_proj(xp, gpre, wlat, gq, wuq, gkv, tabs_p, wuk, (wuv_aug, v_ones),
                                              prompt=True, tm=tm_p, tab_blocks=sp // tm_p)
    z_p, xsc_p, bc_p, dt_p, sga_p, sgb_p, conv_p = _ssm_proj(xp, gpre, wssm, wconv, bconv, tm=min(256, sp), seq=sp,
                                                             name="ssm_proj_prompt")
    oa_p = _attn_prompt(qh_p.reshape(bp, sp, -1), kh_p.reshape(bp, sp, -1), v_p.reshape(bp, sp, -1),
                        tq=min(256, sp))
    os_p, ssm_p = _ssd_prompt(xsc_p, bc_p, z_p, dt_p, dtb, alog, dskip, gssm, batch=bp, seq=sp)
    y_p = _post(xp, oa_p.reshape(bp * sp, -1), os_p, sga_p, sgb_p, *post_w, tm=min(256, sp), name="post_prompt")

    xs_in = x_sample.reshape(bs, D_MODEL)
    tabs_s = _rope_tables(jnp.full((bs,), past, dtype=jnp.int32))
    qh_s, ckv_s, kpe_s, qlat_s = _mla_proj(xs_in, gpre, wlat, gq, wuq, gkv, tabs_s, wuk, None,
                                           prompt=False, tm=bs, tab_blocks=1)
    z_s, xbc_s, dt_s, sga_s, sgb_s = _ssm_proj(xs_in, gpre, wssm, tm=bs, name="ssm_proj_sample")
    o_lat = _attn_sample(page_table, qlat_s, qh_s, ckv_s, kpe_s, cache_kv_latent[l],
                         jnp.swapaxes(cache_k_rope[l], 1, 2))
    oa_s = _uv_project(o_lat.reshape(bs, MLA_HEADS * KV_LORA), wuv)
    cnew_s, xs_s, xdt_t, bm_s, cm_s, dec_s = _ssd_sample_pre(
        xbc_s, jnp.swapaxes(state_conv[l], 0, 1), dt_s, wconv, bconv, dtb, alog)
    ssm_s, os_s = _ssd_sample_state(dec_s[:, :SSM_HEADS], state_ssm[l].reshape(bs, D_INNER, D_STATE),
                                    xdt_t, bm_s, cm_s, xs_s, z_s, dskip, gssm)
    y_s = _post(xs_in, oa_s, os_s, sga_s, sgb_s, *post_w, tm=bs, name="post_sample")

    hshape = (SSM_HEADS, SSM_HEAD_DIM, D_STATE)
    return (y_p.reshape(bp, sp, D_MODEL), y_s.reshape(bs, 1, D_MODEL),
            ckv_p.reshape(1, bp, sp, KV_LORA), jnp.swapaxes(kpe_p, 1, 2)[None],
            ssm_p.reshape(1, bp, *hshape), conv_p.reshape(1, bp, CONV_W - 1, CONV_DIM),
            ckv_s.reshape(1, bs, 1, KV_LORA), kpe_s.reshape(1, bs, 1, ROPE_DIM),
            ssm_s.reshape(1, bs, *hshape), jnp.swapaxes(cnew_s, 0, 1)[None])
```
